```python
import jax, jax.numpy as jnp
from jax import lax
import numpy as np

D_MODEL = 1024
BATCH = 4
SEQ = 4096
DEPTH = 1
DEC_BATCH = 128
DEC_SEQ = 4
PAST_LEN = 2048
PAGE_SIZE = 128

PLE_DIM = 256
D_FF = 2816
GLA_HEADS = 4
GLA_DK = 64
GLA_DV = 128
GLA_RANK = 16
GLA_GATE_TEMP = 16.0
GLA_CHUNK = 64
SB_HEADS = 8
SB_DH = 64
SB_QBLOCK = 128
SB_BIAS_INIT = -8.0
EPS = 1e-6
GLA_QK_W = GLA_HEADS * GLA_DK
GLA_V_W = GLA_HEADS * GLA_DV
SB_W = SB_HEADS * SB_DH
MIX_WIDTH = GLA_V_W + SB_W
IN_SIZES = (GLA_QK_W, GLA_QK_W, GLA_V_W, GLA_V_W, GLA_RANK, SB_W, SB_W, SB_W)
IN_WIDTH = 2 * GLA_QK_W + 2 * GLA_V_W + GLA_RANK + 3 * SB_W

kernel_name = 'hymba_gla_stickbreak_macaron_decoder_step'


def rmsnorm(x, w):
    xf = x.astype(jnp.float32)
    y = xf * lax.rsqrt(jnp.mean(xf * xf, axis=-1, keepdims=True) + EPS) * w.astype(jnp.float32)
    return y.astype(x.dtype)


def swiglu(h, w_in, w_out):
    gu = h @ w_in
    g, u = jnp.split(gu, 2, axis=-1)
    return (jax.nn.silu(g) * u) @ w_out


def project(h, w_in, gate_w, gate_b):
    B, T, _ = h.shape
    z = h @ w_in
    parts = []
    off = 0
    for s in IN_SIZES:
        parts.append(z[..., off:off + s])
        off += s
    gq, gk, gv, gg, glr, sq, sk, sv = parts
    gq = gq.astype(jnp.float32).reshape(B, T, GLA_HEADS, GLA_DK) * GLA_DK ** -0.5
    gk = gk.astype(jnp.float32).reshape(B, T, GLA_HEADS, GLA_DK)
    gv = gv.astype(jnp.float32).reshape(B, T, GLA_HEADS, GLA_DV)
    la = jax.nn.log_sigmoid((glr @ gate_w + gate_b).astype(jnp.float32)) / GLA_GATE_TEMP
    la = la.reshape(B, T, GLA_HEADS, GLA_DK)
    sq = sq.reshape(B, T, SB_HEADS, SB_DH)
    sk = sk.reshape(B, T, SB_HEADS, SB_DH)
    sv = sv.reshape(B, T, SB_HEADS, SB_DH)
    return gq, gk, gv, gg, la, sq, sk, sv


def gla_chunk_step(S, chunk):
    q, k, v, la = chunk
    C = q.shape[1]
    b = jnp.cumsum(la, axis=1)
    causal = jnp.tril(jnp.ones((C, C), dtype=bool))
    diff = b[:, :, None] - b[:, None, :]
    decay = jnp.exp(jnp.where(causal[None, :, :, None, None], diff, -jnp.inf))
    att = jnp.einsum('bthd,bshd,btshd->bhts', q, k, decay)
    o = (jnp.einsum('bhts,bshv->bthv', att, v)
         + jnp.einsum('bthd,bhdv->bthv', q * jnp.exp(b), S))
    b_last = b[:, -1]
    S_new = (jnp.exp(b_last)[..., None] * S
             + jnp.einsum('bshd,bshv->bhdv', k * jnp.exp(b_last[:, None] - b), v))
    return S_new, o


def gla_prompt(q, k, v, la):
    B, T = q.shape[0], q.shape[1]
    n = T // GLA_CHUNK

    def to_chunks(a):
        return a.reshape((B, n, GLA_CHUNK) + a.shape[2:]).swapaxes(0, 1)

    S0 = jnp.zeros((B, GLA_HEADS, GLA_DK, GLA_DV), jnp.float32)
    S, o = lax.scan(gla_chunk_step, S0, (to_chunks(q), to_chunks(k), to_chunks(v), to_chunks(la)))
    o = o.swapaxes(0, 1).reshape(B, T, GLA_HEADS, GLA_DV)
    return o, S


def sb_attend(q, k, v, q_pos, k_pos, bias):
    z = (jnp.einsum('bqhd,bkhd->bhqk', q.astype(jnp.float32), k.astype(jnp.float32)) * SB_DH ** -0.5
         + bias.astype(jnp.float32)[None, :, None, None])
    valid = (k_pos[None, :] < q_pos[:, None])[None, None]
    log_stay = jnp.where(valid, jax.nn.log_sigmoid(-z), 0.0)
    after = lax.cumsum(log_stay, axis=3, reverse=True) - log_stay
    w = jnp.where(valid, jnp.exp(jax.nn.log_sigmoid(z) + after), 0.0)
    return jnp.einsum('bhqk,bkhd->bqhd', w, v.astype(jnp.float32))


def sb_prompt(q, k, v, bias):
    B, T = q.shape[0], q.shape[1]
    nb = T // SB_QBLOCK
    qb = q.reshape(B, nb, SB_QBLOCK, SB_HEADS, SB_DH).swapaxes(0, 1)
    qpos = jnp.arange(T).reshape(nb, SB_QBLOCK)
    kpos = jnp.arange(T)
    o = lax.map(lambda a: sb_attend(a[0], k, v, a[1], kpos, bias), (qb, qpos))
    return o.swapaxes(0, 1).reshape(B, T, SB_HEADS, SB_DH)


def sb_sample(q, k, v, pool_k, pool_v, page_table, bias):
    Bd, Tn = q.shape[0], q.shape[1]
    past_k = pool_k[page_table].reshape(Bd, -1, SB_HEADS, SB_DH)
    past_v = pool_v[page_table].reshape(Bd, -1, SB_HEADS, SB_DH)
    past_len = past_k.shape[1]
    k_all = jnp.concatenate([past_k, k.astype(past_k.dtype)], axis=1)
    v_all = jnp.concatenate([past_v, v.astype(past_v.dtype)], axis=1)
    q_pos = past_len + jnp.arange(Tn)
    k_pos = jnp.arange(past_len + Tn)
    return sb_attend(q, k_all, v_all, q_pos, k_pos, bias)


def merge(o_gla, g_out, o_sb, gla_norm, sb_norm, w_out, dtype):
    B, T = o_gla.shape[0], o_gla.shape[1]
    a = rmsnorm(o_gla, gla_norm).reshape(B, T, GLA_V_W) * jax.nn.silu(g_out.astype(jnp.float32))
    s = rmsnorm(o_sb, sb_norm).reshape(B, T, SB_W)
    m = jnp.concatenate([a, s], axis=-1).astype(dtype)
    return m @ w_out


def layer(x, p, lp, mix_fn):
    x = x + 0.5 * rmsnorm(swiglu(rmsnorm(x, lp['f1_pre']), lp['f1_in'], lp['f1_out']), lp['f1_post'])
    h = rmsnorm(x, lp['mix_pre'])
    gq, gk, gv, gg, la, sq, sk, sv = project(h, lp['w_in'], lp['gate_w'], lp['gate_b'])
    o_gla, o_sb, new_state = mix_fn(gq, gk, gv, la, sq, sk, sv)
    m = merge(o_gla, gg, o_sb, lp['gla_norm'], lp['sb_norm'], lp['w_out'], x.dtype)
    x = x + rmsnorm(m, lp['mix_post'])
    x = x + 0.5 * rmsnorm(swiglu(rmsnorm(x, lp['f2_pre']), lp['f2_in'], lp['f2_out']), lp['f2_post'])
    gate = jax.nn.sigmoid(rmsnorm(x, lp['ple_pre']) @ lp['ple_w_gate'])
    e = p.astype(x.dtype) @ lp['ple_w_up']
    x = x + rmsnorm(gate * e, lp['ple_post'])
    return x, new_state


def setup_inputs(seed: int = 0) -> dict:
    key = jax.random.key(seed)
    ks = jax.random.split(key, 40)
    n_pages = PAST_LEN // PAGE_SIZE
    used = DEC_BATCH * n_pages
    n_pool = used + used // 4

    def nrm(k, shape, scale=1.0):
        return jax.random.normal(k, shape, jnp.float32) * scale

    def gain(k, n):
        return 1.0 + nrm(k, (DEPTH, n), 0.05)

    page_table = jax.random.permutation(ks[0], n_pool)[:used].reshape(DEC_BATCH, n_pages).astype(jnp.int32)
    return {
        'x_prompt': nrm(ks[1], (BATCH, SEQ, D_MODEL)),
        'x_sample': nrm(ks[2], (DEC_BATCH, DEC_SEQ, D_MODEL)),
        'cache_sb_k': nrm(ks[3], (DEPTH, n_pool, PAGE_SIZE, SB_HEADS, SB_DH)),
        'cache_sb_v': nrm(ks[4], (DEPTH, n_pool, PAGE_SIZE, SB_HEADS, SB_DH)),
        'state_gla': nrm(ks[5], (DEPTH, DEC_BATCH, GLA_HEADS, GLA_DK, GLA_DV)),
        'page_table': page_table,
        'p_prompt': nrm(ks[6], (DEPTH, BATCH, SEQ, PLE_DIM)),
        'p_sample': nrm(ks[7], (DEPTH, DEC_BATCH, DEC_SEQ, PLE_DIM)),
        'f1_pre': gain(ks[8], D_MODEL),
        'f1_in': nrm(ks[9], (DEPTH, D_MODEL, 2 * D_FF), D_MODEL ** -0.5),
        'f1_out': nrm(ks[10], (DEPTH, D_FF, D_MODEL), D_FF ** -0.5),
        'f1_post': gain(ks[11], D_MODEL),
        'mix_pre': gain(ks[12], D_MODEL),
        'w_in': nrm(ks[13], (DEPTH, D_MODEL, IN_WIDTH), D_MODEL ** -0.5),
        'gate_w': nrm(ks[14], (DEPTH, GLA_RANK, GLA_QK_W), GLA_RANK ** -0.5),
        'gate_b': nrm(ks[15], (DEPTH, GLA_QK_W), 0.1),
        'gla_norm': gain(ks[16], GLA_DV),
        'sb_norm': gain(ks[17], SB_DH),
        'sb_bias': SB_BIAS_INIT + nrm(ks[28], (DEPTH, SB_HEADS), 0.1),
        'w_out': nrm(ks[18], (DEPTH, MIX_WIDTH, D_MODEL), MIX_WIDTH ** -0.5),
        'mix_post': gain(ks[19], D_MODEL),
        'f2_pre': gain(ks[20], D_MODEL),
        'f2_in': nrm(ks[21], (DEPTH, D_MODEL, 2 * D_FF), D_MODEL ** -0.5),
        'f2_out': nrm(ks[22], (DEPTH, D_FF, D_MODEL), D_FF ** -0.5),
        'f2_post': gain(ks[23], D_MODEL),
        'ple_pre': gain(ks[24], D_MODEL),
        'ple_w_gate': nrm(ks[25], (DEPTH, D_MODEL, D_MODEL), D_MODEL ** -0.5),
        'ple_w_up': nrm(ks[26], (DEPTH, PLE_DIM, D_MODEL), PLE_DIM ** -0.5),
        'ple_post': gain(ks[27], D_MODEL),
    }


def reference(x_prompt, x_sample, cache_sb_k, cache_sb_v, state_gla, page_table, p_prompt, p_sample,
              f1_pre, f1_in, f1_out, f1_post, mix_pre, w_in, gate_w, gate_b, gla_norm, sb_norm, sb_bias,
              w_out, mix_post, f2_pre, f2_in, f2_out, f2_post, ple_pre, ple_w_gate, ple_w_up, ple_post):
    xp, xs = x_prompt, x_sample
    pk_list, pv_list, ps_list, sk_list, sv_list, ss_list = [], [], [], [], [], []
    for l in range(DEPTH):
        lp = {
            'f1_pre': f1_pre[l], 'f1_in': f1_in[l], 'f1_out': f1_out[l], 'f1_post': f1_post[l],
            'mix_pre': mix_pre[l], 'w_in': w_in[l], 'gate_w': gate_w[l], 'gate_b': gate_b[l],
            'gla_norm': gla_norm[l], 'sb_norm': sb_norm[l], 'w_out': w_out[l], 'mix_post': mix_post[l],
            'f2_pre': f2_pre[l], 'f2_in': f2_in[l], 'f2_out': f2_out[l], 'f2_post': f2_post[l],
            'ple_pre': ple_pre[l], 'ple_w_gate': ple_w_gate[l], 'ple_w_up': ple_w_up[l],
            'ple_post': ple_post[l],
        }
        bias_l = sb_bias[l]

        def mix_prompt(gq, gk, gv, la, sq, sk, sv, bias_l=bias_l):
            o_gla, S = gla_prompt(gq, gk, gv, la)
            o_sb = sb_prompt(sq, sk, sv, bias_l)
            return o_gla, o_sb, (sk, sv, S)

        def mix_sample(gq, gk, gv, la, sq, sk, sv, l=l, bias_l=bias_l):
            S, o_gla = gla_chunk_step(state_gla[l].astype(jnp.float32), (gq, gk, gv, la))
            o_sb = sb_sample(sq, sk, sv, cache_sb_k[l], cache_sb_v[l], page_table, bias_l)
            return o_gla, o_sb, (sk, sv, S)

        xp, (pk, pv, ps) = layer(xp, p_prompt[l], lp, mix_prompt)
        xs, (sk_, sv_, ss) = layer(xs, p_sample[l], lp, mix_sample)
        pk_list.append(pk); pv_list.append(pv); ps_list.append(ps)
        sk_list.append(sk_); sv_list.append(sv_); ss_list.append(ss)
    return (xp, xs, jnp.stack(pk_list), jnp.stack(pv_list), jnp.stack(ps_list),
            jnp.stack(sk_list), jnp.stack(sv_list), jnp.stack(ss_list))
```

```python
import functools

import numpy as np
import jax
import jax.numpy as jnp
from jax import lax
from jax.experimental import pallas as pl
from jax.experimental.pallas import tpu as pltpu

F32 = jnp.float32
BF16 = jnp.bfloat16

D_MODEL = 1024
D_FF = 2816
PLE_DIM = 256
GLA_HEADS = 4
GLA_DK = 64
GLA_DV = 128
GLA_RANK = 16
GLA_GATE_TEMP = 16.0
GLA_CHUNK = 64
GLA_SUB = 16
SB_HEADS = 8
SB_DH = 64
PAGE_SIZE = 128
EPS = 1e-6
GLA_QK_W = GLA_HEADS * GLA_DK
GLA_V_W = GLA_HEADS * GLA_DV
SB_W = SB_HEADS * SB_DH
ZG_W = 2 * GLA_QK_W + 2 * GLA_V_W
LANES = 128
SUBLANES = 8
EXP_CLAMP = 80.0

TOK_TILE = 512
FF_TILE = 256
VMEM_LIMIT = 56 * 1024 * 1024


def _log2(n):
    k = int(n).bit_length() - 1
    assert 1 << k == n, n
    return k


def _div(x, n):
    return lax.shift_right_logical(x, _log2(n))


def _mod(x, n):
    assert n & (n - 1) == 0
    return x & (n - 1)


def _iota(shape, dim):
    return lax.broadcasted_iota(jnp.int32, shape, dim)


def _dot(a, b):
    return jnp.dot(a, b, preferred_element_type=F32)


def _dot_nt(a, b):
    return lax.dot_general(a, b, (((1,), (1,)), ((), ())), preferred_element_type=F32)


def _dot01(t_bf16, x):
    hi = x.astype(BF16)
    lo = (x - hi.astype(F32)).astype(BF16)
    return _dot(t_bf16, hi) + _dot(t_bf16, lo)


def _rms(x, w):
    ms = jnp.mean(x * x, axis=-1, keepdims=True)
    return x * lax.rsqrt(ms + EPS) * w


def _sigmoid(x):
    return 1.0 / (1.0 + jnp.exp(-x))


def _softplus(x):
    return jnp.maximum(x, 0.0) + jnp.log1p(jnp.exp(-jnp.abs(x)))


def _ffn_in_body(x_ref, pre_ref, wg_ref, wu_ref, wo_ref, post_ref, mixpre_ref, wcat_ref, gw_ref, gb_ref,
                 x1_ref, zg_ref, la_ref, sq_ref, sk_ref, sv_ref, skb_ref, svb_ref, h_ref, acc_ref):
    j = pl.program_id(1)

    @pl.when(j == 0)
    def _():
        h_ref[...] = _rms(x_ref[...], pre_ref[...]).astype(BF16)
        acc_ref[...] = jnp.zeros_like(acc_ref)

    h = h_ref[...]
    g = _dot(h, wg_ref[...])
    u = _dot(h, wu_ref[...])
    act = (g * _sigmoid(g) * u).astype(BF16)
    acc_ref[...] += _dot(act, wo_ref[...])

    @pl.when(j == pl.num_programs(1) - 1)
    def _():
        x1 = x_ref[...] + 0.5 * _rms(acc_ref[...], post_ref[...])
        x1_ref[...] = x1
        h2 = _rms(x1, mixpre_ref[...]).astype(BF16)
        zg_ref[...] = _dot(h2, wcat_ref[:, 0:ZG_W])
        sb = _dot(h2, wcat_ref[:, ZG_W:ZG_W + 3 * SB_W])
        sq_ref[...] = (sb[:, 0:SB_W] * SB_DH ** -0.5).astype(BF16)
        sk = sb[:, SB_W:2 * SB_W]
        sv = sb[:, 2 * SB_W:3 * SB_W]
        sk_ref[...] = sk
        sv_ref[...] = sv
        skb_ref[...] = sk.astype(BF16)
        svb_ref[...] = sv.astype(BF16)
        lr = _dot(h2, wcat_ref[:, ZG_W + 3 * SB_W:])
        logit = jnp.dot(lr, gw_ref[...], precision=lax.Precision.HIGHEST,
                        preferred_element_type=F32) + gb_ref[...]
        la_ref[...] = (jnp.minimum(logit, 0.0) - jnp.log1p(jnp.exp(-jnp.abs(logit)))) * (1.0 / GLA_GATE_TEMP)


def _ffn_in(x, pre, w_in_b, w_out_b, post, mixpre, wcat_b, gw_pad, gb):
    n = x.shape[0]
    nt, nf = n // TOK_TILE, D_FF // FF_TILE
    row = lambda i, j: (i, 0)
    const = lambda i, j: (0, 0)
    out_shapes = (
        jax.ShapeDtypeStruct((n, D_MODEL), F32),
        jax.ShapeDtypeStruct((n, ZG_W), F32),
        jax.ShapeDtypeStruct((n, GLA_QK_W), F32),
        jax.ShapeDtypeStruct((n, SB_W), BF16),
        jax.ShapeDtypeStruct((n, SB_W), F32),
        jax.ShapeDtypeStruct((n, SB_W), F32),
        jax.ShapeDtypeStruct((n, SB_W), BF16),
        jax.ShapeDtypeStruct((n, SB_W), BF16),
    )
    return pl.pallas_call(
        _ffn_in_body,
        grid=(nt, nf),
        in_specs=[
            pl.BlockSpec((TOK_TILE, D_MODEL), row),
            pl.BlockSpec((1, D_MODEL), const),
            pl.BlockSpec((D_MODEL, FF_TILE), lambda i, j: (0, j)),
            pl.BlockSpec((D_MODEL, FF_TILE), lambda i, j: (0, j + D_FF // FF_TILE)),
            pl.BlockSpec((FF_TILE, D_MODEL), lambda i, j: (j, 0)),
            pl.BlockSpec((1, D_MODEL), const),
            pl.BlockSpec((1, D_MODEL), const),
            pl.BlockSpec(wcat_b.shape, const),
            pl.BlockSpec(gw_pad.shape, const),
            pl.BlockSpec((1, GLA_QK_W), const),
        ],
        out_specs=[
            pl.BlockSpec((TOK_TILE, D_MODEL), row),
            pl.BlockSpec((TOK_TILE, ZG_W), row),
            pl.BlockSpec((TOK_TILE, GLA_QK_W), row),
            pl.BlockSpec((TOK_TILE, SB_W), row),
            pl.BlockSpec((TOK_TILE, SB_W), row),
            pl.BlockSpec((TOK_TILE, SB_W), row),
            pl.BlockSpec((TOK_TILE, SB_W), row),
            pl.BlockSpec((TOK_TILE, SB_W), row),
        ],
        out_shape=out_shapes,
        scratch_shapes=[pltpu.VMEM((TOK_TILE, D_MODEL), BF16), pltpu.VMEM((TOK_TILE, D_MODEL), F32)],
        compiler_params=pltpu.CompilerParams(dimension_semantics=("arbitrary", "arbitrary"),
                                             vmem_limit_bytes=VMEM_LIMIT),
        name="ffn_in",
    )(x, pre, w_in_b, w_in_b, w_out_b, post, mixpre, wcat_b, gw_pad, gb)


def _merge_out_body(x1_ref, a_ref, s_ref, p_ref, wout_ref, mixpost_ref, pre_ref, wg_ref, wu_ref, wo_ref,
                    post_ref, plepre_ref, pleg_ref, pleu_ref, plepost_ref,
                    y_ref, x2_ref, h_ref, acc_ref):
    j = pl.program_id(1)

    @pl.when(j == 0)
    def _():
        m = _dot(a_ref[...], wout_ref[0:GLA_V_W, :]) + _dot(s_ref[...], wout_ref[GLA_V_W:, :])
        x2 = x1_ref[...] + _rms(m, mixpost_ref[...])
        x2_ref[...] = x2
        h_ref[...] = _rms(x2, pre_ref[...]).astype(BF16)
        acc_ref[...] = jnp.zeros_like(acc_ref)

    h = h_ref[...]
    g = _dot(h, wg_ref[...])
    u = _dot(h, wu_ref[...])
    act = (g * _sigmoid(g) * u).astype(BF16)
    acc_ref[...] += _dot(act, wo_ref[...])

    @pl.when(j == pl.num_programs(1) - 1)
    def _():
        x3 = x2_ref[...] + 0.5 * _rms(acc_ref[...], post_ref[...])
        hg = _rms(x3, plepre_ref[...]).astype(BF16)
        gate = _sigmoid(_dot(hg, pleg_ref[...]))
        e = _dot(p_ref[...].astype(BF16), pleu_ref[...])
        y_ref[...] = x3 + _rms(gate * e, plepost_ref[...])


def _merge_out(x1, a, s, p, wout_b, mixpost, pre, w_in_b, w_out_b, post, plepre, pleg_b, pleu_b, plepost):
    n = x1.shape[0]
    nt, nf = n // TOK_TILE, D_FF // FF_TILE
    row = lambda i, j: (i, 0)
    const = lambda i, j: (0, 0)
    vec = pl.BlockSpec((1, D_MODEL), const)
    return pl.pallas_call(
        _merge_out_body,
        grid=(nt, nf),
        in_specs=[
            pl.BlockSpec((TOK_TILE, D_MODEL), row),
            pl.BlockSpec((TOK_TILE, GLA_V_W), row),
            pl.BlockSpec((TOK_TILE, SB_W), row),
            pl.BlockSpec((TOK_TILE, PLE_DIM), row),
            pl.BlockSpec(wout_b.shape, const),
            vec, vec,
            pl.BlockSpec((D_MODEL, FF_TILE), lambda i, j: (0, j)),
            pl.BlockSpec((D_MODEL, FF_TILE), lambda i, j: (0, j + D_FF // FF_TILE)),
            pl.BlockSpec((FF_TILE, D_MODEL), lambda i, j: (j, 0)),
            vec, vec,
            pl.BlockSpec(pleg_b.shape, const),
            pl.BlockSpec(pleu_b.shape, const),
            vec,
        ],
        out_specs=pl.BlockSpec((TOK_TILE, D_MODEL), row),
        out_shape=jax.ShapeDtypeStruct((n, D_MODEL), F32),
        scratch_shapes=[pltpu.VMEM((TOK_TILE, D_MODEL), F32), pltpu.VMEM((TOK_TILE, D_MODEL), BF16),
                        pltpu.VMEM((TOK_TILE, D_MODEL), F32)],
        compiler_params=pltpu.CompilerParams(dimension_semantics=("arbitrary", "arbitrary"),
                                             vmem_limit_bytes=VMEM_LIMIT),
        name="merge_out",
    )(x1, a, s, p, wout_b, mixpost, pre, w_in_b, w_in_b, w_out_b, post, plepre, pleg_b, pleu_b, plepost)


GLA_BLOCK = 256
GLA_PAIR = 128


def _cum_matrices(block, chunk, sub=None):
    t = np.arange(block)[:, None]
    s = np.arange(block)[None, :]
    same = (t // chunk) == (s // chunk)
    mats = [same & (s <= t)]
    if sub is not None:
        mats.append(same & ((s % chunk) < sub * ((t % chunk) // sub)))
    mats.append(same)
    return np.concatenate(mats, axis=0).astype(np.float32)


def _gla_prompt_body(zg_ref, la_ref, norm_ref, cum_ref, a_ref, sout_ref, st_ref):
    i = pl.program_id(1)

    @pl.when(i == 0)
    def _():
        st_ref[...] = jnp.zeros_like(st_ref)

    nb = GLA_BLOCK
    cums = _dot01(cum_ref[...], la_ref[...])
    b = cums[0:nb]
    r = cums[nb:2 * nb]
    bl = cums[2 * nb:3 * nb]
    q = zg_ref[:, 0:GLA_QK_W] * GLA_DK ** -0.5
    k = zg_ref[:, GLA_QK_W:2 * GLA_QK_W]
    q_intra = q * jnp.exp(b - r)
    q_inter = q * jnp.exp(b)
    k_state = (k * jnp.exp(bl - b)).astype(BF16)
    decay = jnp.exp(bl)

    C, S = GLA_CHUNK, GLA_SUB
    n_sub = C // S
    hc = GLA_HEADS * C
    head_sel = _div(_iota((hc, GLA_QK_W), 1), GLA_DK) == _div(_iota((hc, GLA_QK_W), 0), C)
    tt = _mod(_iota((hc, n_sub * C), 0), C)
    cc = _iota((hc, n_sub * C), 1)
    att_keep = (_div(cc, C) == _div(tt, S)) & (_mod(cc, C) <= tt)
    srow = _iota((C, GLA_QK_W), 0)
    pair_chunk = _div(_iota((GLA_DV, GLA_PAIR), 1), C)
    st_lane_head = _div(_iota((GLA_DV, GLA_QK_W), 1), GLA_DK)

    for c in range(nb // C):
        rows = slice(c * C, (c + 1) * C)
        pair = slice((c // 2) * GLA_PAIR, (c // 2 + 1) * GLA_PAIR)
        b_c, k_c = b[rows], k[rows]
        kst = []
        for sidx in range(n_sub):
            r_row = r[c * C + sidx * S:c * C + sidx * S + 1, :]
            kk = k_c * jnp.exp(jnp.minimum(r_row - b_c, EXP_CLAMP))
            kst.append(jnp.where(srow < (sidx + 1) * S, kk, 0.0))
        kst = jnp.concatenate(kst, axis=0).astype(BF16)
        qst = jnp.where(head_sel, jnp.concatenate([q_intra[rows]] * GLA_HEADS, axis=0), 0.0).astype(BF16)
        att = jnp.where(att_keep, _dot_nt(qst, kst), 0.0).astype(BF16)
        qin = jnp.where(head_sel, jnp.concatenate([q_inter[rows]] * GLA_HEADS, axis=0), 0.0).astype(BF16)
        o_inter = _dot_nt(qin, st_ref[...].astype(BF16))
        upd = jnp.zeros((GLA_DV, GLA_QK_W), F32)
        for h in range(GLA_HEADS):
            vcol = slice(2 * GLA_QK_W + h * GLA_DV, 2 * GLA_QK_W + (h + 1) * GLA_DV)
            gcol = slice(2 * GLA_QK_W + GLA_V_W + h * GLA_DV, 2 * GLA_QK_W + GLA_V_W + (h + 1) * GLA_DV)
            v_h = zg_ref[rows, vcol].astype(BF16)
            o = _dot(att[h * C:(h + 1) * C], jnp.concatenate([v_h] * n_sub, axis=0)) + o_inter[h * C:(h + 1) * C]
            gate = zg_ref[rows, gcol]
            a_ref[rows, h * GLA_DV:(h + 1) * GLA_DV] = (
                _rms(o, norm_ref[...]) * (gate * _sigmoid(gate))).astype(BF16)
            v_t = zg_ref[pair, vcol].T
            u_t = _dot(jnp.where(pair_chunk == (c % 2), v_t, 0.0).astype(BF16), k_state[pair])
            upd = upd + jnp.where(st_lane_head == h, u_t, 0.0)
        st_ref[...] = st_ref[...] * decay[c * C:c * C + 1, :] + upd

    @pl.when(i == pl.num_programs(1) - 1)
    def _():
        sout_ref[...] = st_ref[...].T.reshape(1, GLA_HEADS, GLA_DK, GLA_DV)


def _gla_prompt(zg, la, norm, batch, seq):
    nblk = seq // GLA_BLOCK
    cum = jnp.asarray(_cum_matrices(GLA_BLOCK, GLA_CHUNK, GLA_SUB), BF16)
    row = lambda b, i: (b * nblk + i, 0)
    return pl.pallas_call(
        _gla_prompt_body,
        grid=(batch, nblk),
        in_specs=[
            pl.BlockSpec((GLA_BLOCK, ZG_W), row),
            pl.BlockSpec((GLA_BLOCK, GLA_QK_W), row),
            pl.BlockSpec((1, GLA_DV), lambda b, i: (0, 0)),
            pl.BlockSpec(cum.shape, lambda b, i: (0, 0)),
        ],
        out_specs=[
            pl.BlockSpec((GLA_BLOCK, GLA_V_W), row),
            pl.BlockSpec((1, GLA_HEADS, GLA_DK, GLA_DV), lambda b, i: (b, 0, 0, 0)),
        ],
        out_shape=(jax.ShapeDtypeStruct((batch * seq, GLA_V_W), BF16),
                   jax.ShapeDtypeStruct((batch, GLA_HEADS, GLA_DK, GLA_DV), F32)),
        scratch_shapes=[pltpu.VMEM((GLA_DV, GLA_QK_W), F32)],
        compiler_params=pltpu.CompilerParams(dimension_semantics=("arbitrary", "arbitrary"),
                                             vmem_limit_bytes=VMEM_LIMIT),
        name="gla_prompt",
    )(zg, la, norm, cum)


GLA_S_ROWS = 128


def _gla_sample_body(n_new, zg_ref, la_ref, s_ref, norm_ref, cum_ref, a_ref, sout_ref,
                     qin_ref, khat_ref, dec_ref, oint_ref, vt_ref):
    rows = GLA_S_ROWS
    cums = _dot01(cum_ref[...], la_ref[...])
    b = cums[0:rows]
    bl = cums[rows:2 * rows]
    q = zg_ref[:, 0:GLA_QK_W] * GLA_DK ** -0.5
    k = zg_ref[:, GLA_QK_W:2 * GLA_QK_W]
    q_dec = q * jnp.exp(b)
    k_intra = (k * jnp.exp(jnp.minimum(-b, EXP_CLAMP))).astype(BF16)
    qin_ref[...] = q_dec
    khat_ref[...] = (k * jnp.exp(bl - b)).astype(BF16)
    dec_ref[...] = jnp.exp(bl)

    hr = GLA_HEADS * rows
    head_sel = _div(_iota((hr, GLA_QK_W), 1), GLA_DK) == _div(_iota((hr, GLA_QK_W), 0), rows)
    qst = jnp.where(head_sel, jnp.concatenate([q_dec] * GLA_HEADS, axis=0), 0.0).astype(BF16)
    tt = _mod(_iota((hr, rows), 0), rows)
    ss = _iota((hr, rows), 1)
    keep = (_div(tt, n_new) == _div(ss, n_new)) & (ss <= tt)
    att = jnp.where(keep, _dot_nt(qst, k_intra), 0.0).astype(BF16)
    for h in range(GLA_HEADS):
        vcol = slice(2 * GLA_QK_W + h * GLA_DV, 2 * GLA_QK_W + (h + 1) * GLA_DV)
        v_h = zg_ref[:, vcol]
        oint_ref[:, h * GLA_DV:(h + 1) * GLA_DV] = _dot(att[h * rows:(h + 1) * rows], v_h.astype(BF16))
        vt_ref[h] = v_h.T

    per8 = SUBLANES // n_new
    sub8 = _div(_iota((SUBLANES, GLA_QK_W), 0), n_new)
    lane8 = _div(_iota((SUBLANES, GLA_QK_W), 1), GLA_DK)
    out8 = _div(_iota((SUBLANES, GLA_DV), 0), n_new)
    vt_seq = _div(_iota((GLA_DV, rows), 1), n_new)
    st_lane_head = _div(_iota((GLA_DV, GLA_QK_W), 1), GLA_DK)

    def group_step(m, carry):
        r8 = pl.ds(pl.multiple_of(m * SUBLANES, SUBLANES), SUBLANES)
        q8 = qin_ref[r8, :]
        d8 = dec_ref[r8, :]
        res = []
        for e in range(per8):
            n = m * per8 + e
            s_n = s_ref[0, n].reshape(GLA_HEADS * GLA_DK, GLA_DV)
            lhs = jnp.concatenate([jnp.where((sub8 == e) & (lane8 == h), q8, 0.0) for h in range(GLA_HEADS)],
                                  axis=0)
            res.append(_dot(lhs.astype(BF16), s_n.astype(BF16)))
            upd = jnp.zeros((GLA_DV, GLA_QK_W), F32)
            for h in range(GLA_HEADS):
                u_t = _dot(jnp.where(vt_seq == n, vt_ref[h], 0.0).astype(BF16), khat_ref[...])
                upd = upd + jnp.where(st_lane_head == h, u_t, 0.0)
            s_new_t = s_n.T * d8[e * n_new:e * n_new + 1, :] + upd
            sout_ref[0, n] = s_new_t.T.reshape(GLA_HEADS, GLA_DK, GLA_DV)
        for h in range(GLA_HEADS):
            o8 = res[0][h * SUBLANES:(h + 1) * SUBLANES]
            for e in range(1, per8):
                o8 = jnp.where(out8 == e, res[e][h * SUBLANES:(h + 1) * SUBLANES], o8)
            oint_ref[r8, h * GLA_DV:(h + 1) * GLA_DV] += o8
        return carry

    lax.fori_loop(0, rows // SUBLANES, group_step, 0)

    for h in range(GLA_HEADS):
        gcol = slice(2 * GLA_QK_W + GLA_V_W + h * GLA_DV, 2 * GLA_QK_W + GLA_V_W + (h + 1) * GLA_DV)
        gate = zg_ref[:, gcol]
        o = oint_ref[:, h * GLA_DV:(h + 1) * GLA_DV]
        a_ref[:, h * GLA_DV:(h + 1) * GLA_DV] = (_rms(o, norm_ref[...]) * (gate * _sigmoid(gate))).astype(BF16)


def _gla_sample(zg, la, state, norm, row0, n_seq, n_new):
    assert SUBLANES % n_new == 0 and GLA_S_ROWS % n_new == 0
    seqs = GLA_S_ROWS // n_new
    steps = n_seq // seqs
    blk0 = row0 // GLA_S_ROWS
    cum = jnp.asarray(_cum_matrices(GLA_S_ROWS, n_new), BF16)
    return pl.pallas_call(
        functools.partial(_gla_sample_body, n_new),
        grid=(steps,),
        in_specs=[
            pl.BlockSpec((GLA_S_ROWS, ZG_W), lambda i: (blk0 + i, 0)),
            pl.BlockSpec((GLA_S_ROWS, GLA_QK_W), lambda i: (blk0 + i, 0)),
            pl.BlockSpec((1, seqs, GLA_HEADS, GLA_DK, GLA_DV), lambda i: (0, i, 0, 0, 0)),
            pl.BlockSpec((1, GLA_DV), lambda i: (0, 0)),
            pl.BlockSpec(cum.shape, lambda i: (0, 0)),
        ],
        out_specs=[
            pl.BlockSpec((GLA_S_ROWS, GLA_V_W), lambda i: (i, 0)),
            pl.BlockSpec((1, seqs, GLA_HEADS, GLA_DK, GLA_DV), lambda i: (0, i, 0, 0, 0)),
        ],
        out_shape=(jax.ShapeDtypeStruct((n_seq * n_new, GLA_V_W), BF16),
                   jax.ShapeDtypeStruct((1, n_seq, GLA_HEADS, GLA_DK, GLA_DV), F32)),
        scratch_shapes=[
            pltpu.VMEM((GLA_S_ROWS, GLA_QK_W), F32),
            pltpu.VMEM((GLA_S_ROWS, GLA_QK_W), BF16),
            pltpu.VMEM((GLA_S_ROWS, GLA_QK_W), F32),
            pltpu.VMEM((GLA_S_ROWS, GLA_V_W), F32),
            pltpu.VMEM((GLA_HEADS, GLA_DV, GLA_S_ROWS), F32),
        ],
        compiler_params=pltpu.CompilerParams(dimension_semantics=("arbitrary",),
                                             vmem_limit_bytes=VMEM_LIMIT),
        name="gla_sample",
    )(zg, la, state, norm, cum)


SB_TILE = 256
SB_PAIR = LANES // SB_DH


def _sb_prompt_body(bias_ref, q_ref, k_ref, v_ref, norm_ref, u_ref, o_ref, acc_ref):
    p = pl.program_id(1)
    qi = pl.program_id(2)
    T = SB_TILE
    lane_head = _div(_iota((T, LANES), 1), SB_DH)
    q = q_ref[...]
    qh = [jnp.where(lane_head == e, q, jnp.zeros_like(q)) for e in range(SB_PAIR)]
    bias = [bias_ref[p * SB_PAIR + e] for e in range(SB_PAIR)]
    u = u_ref[...]
    acc_ref[...] = jnp.zeros_like(acc_ref)
    valid = _iota((T, T), 1) < _iota((T, T), 0)

    def tile(kj, carry, masked):
        rows = pl.ds(pl.multiple_of(kj * T, T), T)
        kt = k_ref[rows, :]
        vt = v_ref[rows, :]
        new = []
        for e in range(SB_PAIR):
            z = _dot_nt(qh[e], kt) + bias[e]
            ls = -_softplus(z)
            if masked:
                ls = jnp.where(valid, ls, 0.0)
            cs = _dot(ls.astype(BF16), u)
            w = jnp.exp(z + ls + cs + carry[e])
            if masked:
                w = jnp.where(valid, w, 0.0)
            acc_ref[e] += _dot(w.astype(BF16), vt)
            new.append(carry[e] + jnp.sum(ls, axis=1, keepdims=True))
        return tuple(new)

    zero = jnp.zeros((T, 1), F32)
    carry = tile(qi, (zero,) * SB_PAIR, True)
    lax.fori_loop(0, qi, lambda s, c: tile(qi - 1 - s, c, False), carry)

    o = acc_ref[0]
    for e in range(1, SB_PAIR):
        o = jnp.where(lane_head == e, acc_ref[e], o)
    sq = o * o
    ms = jnp.zeros_like(o)
    for e in range(SB_PAIR):
        ms_e = jnp.sum(jnp.where(lane_head == e, sq, 0.0), axis=1, keepdims=True) * (1.0 / SB_DH)
        ms = jnp.where(lane_head == e, ms_e, ms)
    o_ref[...] = (o * lax.rsqrt(ms + EPS) * norm_ref[...]).astype(BF16)


def _sb_prompt(bias, sq_b, sk_b, sv_b, norm_pair, batch, seq):
    nq = seq // SB_TILE
    npair = SB_HEADS // SB_PAIR
    j = np.arange(SB_TILE)[:, None]
    s = np.arange(SB_TILE)[None, :]
    u = jnp.asarray((j > s).astype(np.float32), BF16)
    return pl.pallas_call(
        _sb_prompt_body,
        grid=(batch, npair, nq),
        in_specs=[
            pl.BlockSpec(memory_space=pltpu.SMEM),
            pl.BlockSpec((SB_TILE, LANES), lambda b, p, i: (b * nq + i, p)),
            pl.BlockSpec((seq, LANES), lambda b, p, i: (b, p)),
            pl.BlockSpec((seq, LANES), lambda b, p, i: (b, p)),
            pl.BlockSpec((1, LANES), lambda b, p, i: (0, 0)),
            pl.BlockSpec((SB_TILE, SB_TILE), lambda b, p, i: (0, 0)),
        ],
        out_specs=pl.BlockSpec((SB_TILE, LANES), lambda b, p, i: (b * nq + i, p)),
        out_shape=jax.ShapeDtypeStruct((batch * seq, SB_W), BF16),
        scratch_shapes=[pltpu.VMEM((SB_PAIR, SB_TILE, LANES), F32)],
        compiler_params=pltpu.CompilerParams(dimension_semantics=("arbitrary", "arbitrary", "arbitrary"),
                                             vmem_limit_bytes=VMEM_LIMIT),
        name="sb_prompt",
    )(bias, sq_b, sk_b, sv_b, norm_pair, u)


def _sb_scan_matrix():
    c_in = np.arange(LANES)[:, None]
    c_out = np.arange(LANES)[None, :]
    same_head = (c_in % SB_HEADS) == (c_out % SB_HEADS)
    later = (c_in // SB_HEADS) > (c_out // SB_HEADS)
    return np.concatenate([same_head & later, same_head], axis=1).astype(np.float32)


def _sb_sample_body(n_new, n_pages, pt_ref, bias_ref, q_ref, kn_ref, vn_ref, k_ref, v_ref, norm_ref, up_ref,
                    o_ref, acc_ref, carry_ref):
    del pt_ref
    j = pl.program_id(1)
    qrows = n_new * SB_HEADS
    page_cols = PAGE_SIZE * SB_HEADS
    groups = page_cols // LANES
    q = q_ref[0]
    up = up_ref[...]

    def stick_break(z, keep, carry):
        n = z.shape[1] // LANES
        ls = jnp.where(keep, -_softplus(z), 0.0)
        stacked = jnp.concatenate([ls[:, g * LANES:(g + 1) * LANES] for g in range(n)], axis=0)
        r = _dot(stacked.astype(BF16), up)
        after = [None] * n
        run = carry
        for g in reversed(range(n)):
            blk = r[g * qrows:(g + 1) * qrows]
            after[g] = blk[:, 0:LANES] + run
            run = run + blk[:, LANES:2 * LANES]
        after = jnp.concatenate(after, axis=1) if n > 1 else after[0]
        w = jnp.where(keep, jnp.exp(z + ls + after), 0.0)
        return w.astype(BF16), run

    @pl.when(j == 0)
    def _():
        pad = jnp.zeros((LANES - qrows, SB_DH), F32)
        kn = jnp.concatenate([kn_ref[0], pad], axis=0).astype(BF16)
        vn = jnp.concatenate([vn_ref[0], pad], axis=0).astype(BF16)
        z = _dot_nt(q, kn) + bias_ref[:, 0:LANES]
        row, col = _iota((qrows, LANES), 0), _iota((qrows, LANES), 1)
        keep = ((_mod(row, SB_HEADS) == _mod(col, SB_HEADS))
                & (_div(col, SB_HEADS) < _div(row, SB_HEADS)))
        w, run = stick_break(z, keep, jnp.zeros((qrows, LANES), F32))
        acc_ref[...] = _dot(w, vn)
        carry_ref[...] = run

    k2 = k_ref[...].reshape(page_cols, SB_DH).astype(BF16)
    v2 = v_ref[...].reshape(page_cols, SB_DH).astype(BF16)
    z = _dot_nt(q, k2) + bias_ref[...]
    keep = _mod(_iota((qrows, page_cols), 0), SB_HEADS) == _mod(_iota((qrows, page_cols), 1), SB_HEADS)
    w, run = stick_break(z, keep, carry_ref[...])
    acc_ref[...] += _dot(w, v2)
    carry_ref[...] = run

    @pl.when(j == n_pages - 1)
    def _():
        o_ref[0] = _rms(acc_ref[...], norm_ref[...]).astype(BF16)


def _sb_sample(page_table, bias_tiled, q_s, k_new, v_new, pool_k, pool_v, norm, layer, n_new):
    n_seq, n_pages = page_table.shape
    qrows = n_new * SB_HEADS
    assert qrows <= LANES
    up = jnp.asarray(_sb_scan_matrix(), BF16)
    page = lambda b, j, pt: (layer, pt[b * n_pages + n_pages - 1 - j], 0, 0, 0)
    seq3 = lambda b, j, pt: (b, 0, 0)
    const = lambda b, j, pt: (0, 0)
    grid_spec = pltpu.PrefetchScalarGridSpec(
        num_scalar_prefetch=1,
        grid=(n_seq, n_pages),
        in_specs=[
            pl.BlockSpec(bias_tiled.shape, const),
            pl.BlockSpec((1, qrows, SB_DH), seq3),
            pl.BlockSpec((1, qrows, SB_DH), seq3),
            pl.BlockSpec((1, qrows, SB_DH), seq3),
            pl.BlockSpec((None, None, PAGE_SIZE, SB_HEADS, SB_DH), page),
            pl.BlockSpec((None, None, PAGE_SIZE, SB_HEADS, SB_DH), page),
            pl.BlockSpec((1, SB_DH), const),
            pl.BlockSpec(up.shape, const),
        ],
        out_specs=pl.BlockSpec((1, qrows, SB_DH), seq3),
        scratch_shapes=[pltpu.VMEM((qrows, SB_DH), F32), pltpu.VMEM((qrows, LANES), F32)],
    )
    return pl.pallas_call(
        functools.partial(_sb_sample_body, n_new, n_pages),
        grid_spec=grid_spec,
        out_shape=jax.ShapeDtypeStruct((n_seq, qrows, SB_DH), BF16),
        compiler_params=pltpu.CompilerParams(dimension_semantics=("arbitrary", "arbitrary"),
                                             vmem_limit_bytes=VMEM_LIMIT),
        name="sb_sample",
    )(page_table.reshape(-1), bias_tiled, q_s, k_new, v_new, pool_k, pool_v, norm, up)


def kernel(x_prompt, x_sample, cache_sb_k, cache_sb_v, state_gla, page_table, p_prompt, p_sample, f1_pre, f1_in, f1_out, f1_post, mix_pre, w_in, gate_w, gate_b, gla_norm, sb_norm, sb_bias, w_out, mix_post, f2_pre, f2_in, f2_out, f2_post, ple_pre, ple_w_gate, ple_w_up, ple_post):
    batch, seq, _ = x_prompt.shape
    n_seq, n_new, _ = x_sample.shape
    depth = f1_in.shape[0]
    n_p, n_s = batch * seq, n_seq * n_new
    assert n_p % TOK_TILE == 0 and n_s % TOK_TILE == 0 and seq % SB_TILE == 0 and n_p % GLA_S_ROWS == 0

    x = jnp.concatenate([x_prompt.reshape(n_p, D_MODEL), x_sample.reshape(n_s, D_MODEL)], axis=0)
    vec = lambda a: a.reshape(1, -1)
    lr0 = ZG_W
    sb0 = ZG_W + GLA_RANK
    pk, pv, ps, sk_l, sv_l, ss = [], [], [], [], [], []
    for l in range(depth):
        wcat = jnp.concatenate(
            [w_in[l][:, :lr0], w_in[l][:, sb0:], w_in[l][:, lr0:sb0],
             jnp.zeros((D_MODEL, LANES - GLA_RANK), F32)], axis=1).astype(BF16)
        gw_pad = jnp.concatenate([gate_w[l], jnp.zeros((LANES - GLA_RANK, GLA_QK_W), F32)], axis=0)
        x1, zg, la, sq_b, sk, sv, sk_b, sv_b = _ffn_in(
            x, vec(f1_pre[l]), f1_in[l].astype(BF16), f1_out[l].astype(BF16), vec(f1_post[l]),
            vec(mix_pre[l]), wcat, gw_pad, vec(gate_b[l]))

        a_p, s_gla_p = _gla_prompt(zg, la, vec(gla_norm[l]), batch, seq)
        norm_pair = vec(jnp.tile(sb_norm[l], SB_PAIR))
        o_sb_p = _sb_prompt(sb_bias[l], sq_b, sk_b, sv_b, norm_pair, batch, seq)

        a_s, s_gla_s = _gla_sample(zg, la, state_gla[l:l + 1], vec(gla_norm[l]), n_p, n_seq, n_new)
        heads3 = lambda a: a[n_p:].reshape(n_seq, n_new * SB_HEADS, SB_DH)
        bias_tiled = vec(jnp.tile(sb_bias[l], PAGE_SIZE))
        o_sb_s = _sb_sample(page_table, bias_tiled, heads3(sq_b), heads3(sk), heads3(sv),
                            cache_sb_k, cache_sb_v, vec(sb_norm[l]), l, n_new)

        a = jnp.concatenate([a_p, a_s], axis=0)
        s = jnp.concatenate([o_sb_p, o_sb_s.reshape(n_s, SB_W)], axis=0)
        p = jnp.concatenate([p_prompt[l].reshape(n_p, PLE_DIM), p_sample[l].reshape(n_s, PLE_DIM)], axis=0)
        x = _merge_out(x1, a, s, p, w_out[l].astype(BF16), vec(mix_post[l]), vec(f2_pre[l]),
                       f2_in[l].astype(BF16), f2_out[l].astype(BF16), vec(f2_post[l]), vec(ple_pre[l]),
                       ple_w_gate[l].astype(BF16), ple_w_up[l].astype(BF16), vec(ple_post[l]))

        pk.append(sk[:n_p].reshape(batch, seq, SB_HEADS, SB_DH))
        pv.append(sv[:n_p].reshape(batch, seq, SB_HEADS, SB_DH))
        ps.append(s_gla_p)
        sk_l.append(sk[n_p:].reshape(n_seq, n_new, SB_HEADS, SB_DH))
        sv_l.append(sv[n_p:].reshape(n_seq, n_new, SB_HEADS, SB_DH))
        ss.append(s_gla_s[0])
    return (x[:n_p].reshape(batch, seq, D_MODEL), x[n_p:].reshape(n_seq, n_new, D_MODEL),
            jnp.stack(pk), jnp.stack(pv), jnp.stack(ps), jnp.stack(sk_l), jnp.stack(sv_l), jnp.stack(ss))
```

```python
import functools

import numpy as np
import jax
import jax.numpy as jnp
from jax import lax
from jax.experimental import pallas as pl
from jax.experimental.pallas import tpu as pltpu

F32 = jnp.float32
BF16 = jnp.bfloat16

D_MODEL = 1024
D_FF = 2816
PLE_DIM = 256
GLA_HEADS = 4
GLA_DK = 64
GLA_DV = 128
GLA_RANK = 16
GLA_GATE_TEMP = 16.0
GLA_CHUNK = 64
GLA_SUB = 16
SB_HEADS = 8
SB_DH = 64
PAGE_SIZE = 128
EPS = 1e-6
GLA_QK_W = GLA_HEADS * GLA_DK
GLA_V_W = GLA_HEADS * GLA_DV
SB_W = SB_HEADS * SB_DH
ZG_W = 2 * GLA_QK_W + 2 * GLA_V_W
LANES = 128
SUBLANES = 8
EXP_CLAMP = 80.0
LOG2E = 1.4426950408889634

TOK_TILE = 512
FF_TILE = 256
VMEM_LIMIT = 56 * 1024 * 1024


def _log2(n):
    k = int(n).bit_length() - 1
    assert 1 << k == n, n
    return k


def _div(x, n):
    return lax.shift_right_logical(x, _log2(n))


def _mod(x, n):
    assert n & (n - 1) == 0
    return x & (n - 1)


def _iota(shape, dim):
    return lax.broadcasted_iota(jnp.int32, shape, dim)


def _dot(a, b):
    return jnp.dot(a, b, preferred_element_type=F32)


def _dot_nt(a, b):
    return lax.dot_general(a, b, (((1,), (1,)), ((), ())), preferred_element_type=F32)


def _dot01(t_bf16, x):
    hi = x.astype(BF16)
    lo = (x - hi.astype(F32)).astype(BF16)
    return _dot(t_bf16, hi) + _dot(t_bf16, lo)


def _rms(x, w):
    ms = jnp.mean(x * x, axis=-1, keepdims=True)
    return x * lax.rsqrt(ms + EPS) * w


def _sigmoid(x):
    return 1.0 / (1.0 + jnp.exp(-x))


def _softplus2(x):
    return jnp.maximum(x, jnp.log(1.0 + jnp.exp2(jnp.minimum(x, 126.0))) * LOG2E)


def _ffn_in_body(x_ref, pre_ref, wg_ref, wu_ref, wo_ref, post_ref, mixpre_ref, wcat_ref, gw_ref, gb_ref,
                 x1_ref, zg_ref, la_ref, sq_ref, sk_ref, sv_ref, skb_ref, svb_ref, h_ref, acc_ref):
    j = pl.program_id(1)

    @pl.when(j == 0)
    def _():
        h_ref[...] = _rms(x_ref[...], pre_ref[...]).astype(BF16)
        acc_ref[...] = jnp.zeros_like(acc_ref)

    h = h_ref[...]
    g = _dot(h, wg_ref[...])
    u = _dot(h, wu_ref[...])
    act = (g * _sigmoid(g) * u).astype(BF16)
    acc_ref[...] += _dot(act, wo_ref[...])

    @pl.when(j == pl.num_programs(1) - 1)
    def _():
        x1 = x_ref[...] + 0.5 * _rms(acc_ref[...], post_ref[...])
        x1_ref[...] = x1
        h2 = _rms(x1, mixpre_ref[...]).astype(BF16)
        zg_ref[...] = _dot(h2, wcat_ref[:, 0:ZG_W])
        sb = _dot(h2, wcat_ref[:, ZG_W:ZG_W + 3 * SB_W])
        sq_ref[...] = (sb[:, 0:SB_W] * (SB_DH ** -0.5 * LOG2E)).astype(BF16)
        sk = sb[:, SB_W:2 * SB_W]
        sv = sb[:, 2 * SB_W:3 * SB_W]
        sk_ref[...] = sk
        sv_ref[...] = sv
        skb_ref[...] = sk.astype(BF16)
        svb_ref[...] = sv.astype(BF16)
        lr = _dot(h2, wcat_ref[:, ZG_W + 3 * SB_W:])
        logit = jnp.dot(lr, gw_ref[...], precision=lax.Precision.HIGHEST,
                        preferred_element_type=F32) + gb_ref[...]
        la_ref[...] = (jnp.minimum(logit, 0.0) - jnp.log1p(jnp.exp(-jnp.abs(logit)))) * (1.0 / GLA_GATE_TEMP)


def _ffn_in(x, pre, w_in_b, w_out_b, post, mixpre, wcat_b, gw_pad, gb):
    n = x.shape[0]
    nt, nf = n // TOK_TILE, D_FF // FF_TILE
    row = lambda i, j: (i, 0)
    const = lambda i, j: (0, 0)
    out_shapes = (
        jax.ShapeDtypeStruct((n, D_MODEL), F32),
        jax.ShapeDtypeStruct((n, ZG_W), F32),
        jax.ShapeDtypeStruct((n, GLA_QK_W), F32),
        jax.ShapeDtypeStruct((n, SB_W), BF16),
        jax.ShapeDtypeStruct((n, SB_W), F32),
        jax.ShapeDtypeStruct((n, SB_W), F32),
        jax.ShapeDtypeStruct((n, SB_W), BF16),
        jax.ShapeDtypeStruct((n, SB_W), BF16),
    )
    return pl.pallas_call(
        _ffn_in_body,
        grid=(nt, nf),
        in_specs=[
            pl.BlockSpec((TOK_TILE, D_MODEL), row),
            pl.BlockSpec((1, D_MODEL), const),
            pl.BlockSpec((D_MODEL, FF_TILE), lambda i, j: (0, j)),
            pl.BlockSpec((D_MODEL, FF_TILE), lambda i, j: (0, j + D_FF // FF_TILE)),
            pl.BlockSpec((FF_TILE, D_MODEL), lambda i, j: (j, 0)),
            pl.BlockSpec((1, D_MODEL), const),
            pl.BlockSpec((1, D_MODEL), const),
            pl.BlockSpec(wcat_b.shape, const),
            pl.BlockSpec(gw_pad.shape, const),
            pl.BlockSpec((1, GLA_QK_W), const),
        ],
        out_specs=[
            pl.BlockSpec((TOK_TILE, D_MODEL), row),
            pl.BlockSpec((TOK_TILE, ZG_W), row),
            pl.BlockSpec((TOK_TILE, GLA_QK_W), row),
            pl.BlockSpec((TOK_TILE, SB_W), row),
            pl.BlockSpec((TOK_TILE, SB_W), row),
            pl.BlockSpec((TOK_TILE, SB_W), row),
            pl.BlockSpec((TOK_TILE, SB_W), row),
            pl.BlockSpec((TOK_TILE, SB_W), row),
        ],
        out_shape=out_shapes,
        scratch_shapes=[pltpu.VMEM((TOK_TILE, D_MODEL), BF16), pltpu.VMEM((TOK_TILE, D_MODEL), F32)],
        compiler_params=pltpu.CompilerParams(dimension_semantics=("arbitrary", "arbitrary"),
                                             vmem_limit_bytes=VMEM_LIMIT),
        name="ffn_in",
    )(x, pre, w_in_b, w_in_b, w_out_b, post, mixpre, wcat_b, gw_pad, gb)


def _merge_out_body(x1_ref, a_ref, s_ref, p_ref, wout_ref, mixpost_ref, pre_ref, wg_ref, wu_ref, wo_ref,
                    post_ref, plepre_ref, pleg_ref, pleu_ref, plepost_ref,
                    y_ref, x2_ref, h_ref, acc_ref):
    j = pl.program_id(1)

    @pl.when(j == 0)
    def _():
        m = _dot(a_ref[...], wout_ref[0:GLA_V_W, :]) + _dot(s_ref[...], wout_ref[GLA_V_W:, :])
        x2 = x1_ref[...] + _rms(m, mixpost_ref[...])
        x2_ref[...] = x2
        h_ref[...] = _rms(x2, pre_ref[...]).astype(BF16)
        acc_ref[...] = jnp.zeros_like(acc_ref)

    h = h_ref[...]
    g = _dot(h, wg_ref[...])
    u = _dot(h, wu_ref[...])
    act = (g * _sigmoid(g) * u).astype(BF16)
    acc_ref[...] += _dot(act, wo_ref[...])

    @pl.when(j == pl.num_programs(1) - 1)
    def _():
        x3 = x2_ref[...] + 0.5 * _rms(acc_ref[...], post_ref[...])
        hg = _rms(x3, plepre_ref[...]).astype(BF16)
        gate = _sigmoid(_dot(hg, pleg_ref[...]))
        e = _dot(p_ref[...].astype(BF16), pleu_ref[...])
        y_ref[...] = x3 + _rms(gate * e, plepost_ref[...])


def _merge_out(x1, a, s, p, wout_b, mixpost, pre, w_in_b, w_out_b, post, plepre, pleg_b, pleu_b, plepost):
    n = x1.shape[0]
    nt, nf = n // TOK_TILE, D_FF // FF_TILE
    row = lambda i, j: (i, 0)
    const = lambda i, j: (0, 0)
    vec = pl.BlockSpec((1, D_MODEL), const)
    return pl.pallas_call(
        _merge_out_body,
        grid=(nt, nf),
        in_specs=[
            pl.BlockSpec((TOK_TILE, D_MODEL), row),
            pl.BlockSpec((TOK_TILE, GLA_V_W), row),
            pl.BlockSpec((TOK_TILE, SB_W), row),
            pl.BlockSpec((TOK_TILE, PLE_DIM), row),
            pl.BlockSpec(wout_b.shape, const),
            vec, vec,
            pl.BlockSpec((D_MODEL, FF_TILE), lambda i, j: (0, j)),
            pl.BlockSpec((D_MODEL, FF_TILE), lambda i, j: (0, j + D_FF // FF_TILE)),
            pl.BlockSpec((FF_TILE, D_MODEL), lambda i, j: (j, 0)),
            vec, vec,
            pl.BlockSpec(pleg_b.shape, const),
            pl.BlockSpec(pleu_b.shape, const),
            vec,
        ],
        out_specs=pl.BlockSpec((TOK_TILE, D_MODEL), row),
        out_shape=jax.ShapeDtypeStruct((n, D_MODEL), F32),
        scratch_shapes=[pltpu.VMEM((TOK_TILE, D_MODEL), F32), pltpu.VMEM((TOK_TILE, D_MODEL), BF16),
                        pltpu.VMEM((TOK_TILE, D_MODEL), F32)],
        compiler_params=pltpu.CompilerParams(dimension_semantics=("arbitrary", "arbitrary"),
                                             vmem_limit_bytes=VMEM_LIMIT),
        name="merge_out",
    )(x1, a, s, p, wout_b, mixpost, pre, w_in_b, w_in_b, w_out_b, post, plepre, pleg_b, pleu_b, plepost)


GLA_BLOCK = 256
GLA_PAIR = 128


def _cum_matrices(block, chunk, sub=None):
    t = np.arange(block)[:, None]
    s = np.arange(block)[None, :]
    same = (t // chunk) == (s // chunk)
    mats = [same & (s <= t)]
    if sub is not None:
        mats.append(same & ((s % chunk) < sub * ((t % chunk) // sub)))
    mats.append(same)
    return np.concatenate(mats, axis=0).astype(np.float32)


def _gla_prompt_body(zg_ref, la_ref, norm_ref, cum_ref, a_ref, sout_ref, st_ref):
    i = pl.program_id(1)

    @pl.when(i == 0)
    def _():
        st_ref[...] = jnp.zeros_like(st_ref)

    nb = GLA_BLOCK
    cums = _dot01(cum_ref[...], la_ref[...])
    b = cums[0:nb]
    r = cums[nb:2 * nb]
    bl = cums[2 * nb:3 * nb]
    q = zg_ref[:, 0:GLA_QK_W] * GLA_DK ** -0.5
    k = zg_ref[:, GLA_QK_W:2 * GLA_QK_W]
    q_intra = q * jnp.exp(b - r)
    q_inter = q * jnp.exp(b)
    k_state = (k * jnp.exp(bl - b)).astype(BF16)
    decay = jnp.exp(bl)

    C, S = GLA_CHUNK, GLA_SUB
    n_sub = C // S
    hc = GLA_HEADS * C
    head_sel = _div(_iota((hc, GLA_QK_W), 1), GLA_DK) == _div(_iota((hc, GLA_QK_W), 0), C)
    tt = _mod(_iota((hc, n_sub * C), 0), C)
    cc = _iota((hc, n_sub * C), 1)
    att_keep = (_div(cc, C) == _div(tt, S)) & (_mod(cc, C) <= tt)
    srow = _iota((C, GLA_QK_W), 0)
    pair_chunk = _div(_iota((GLA_DV, GLA_PAIR), 1), C)
    st_lane_head = _div(_iota((GLA_DV, GLA_QK_W), 1), GLA_DK)

    for c in range(nb // C):
        rows = slice(c * C, (c + 1) * C)
        pair = slice((c // 2) * GLA_PAIR, (c // 2 + 1) * GLA_PAIR)
        b_c, k_c = b[rows], k[rows]
        kst = []
        for sidx in range(n_sub):
            r_row = r[c * C + sidx * S:c * C + sidx * S + 1, :]
            kk = k_c * jnp.exp(jnp.minimum(r_row - b_c, EXP_CLAMP))
            kst.append(jnp.where(srow < (sidx + 1) * S, kk, 0.0))
        kst = jnp.concatenate(kst, axis=0).astype(BF16)
        qst = jnp.where(head_sel, jnp.concatenate([q_intra[rows]] * GLA_HEADS, axis=0), 0.0).astype(BF16)
        att = jnp.where(att_keep, _dot_nt(qst, kst), 0.0).astype(BF16)
        qin = jnp.where(head_sel, jnp.concatenate([q_inter[rows]] * GLA_HEADS, axis=0), 0.0).astype(BF16)
        o_inter = _dot_nt(qin, st_ref[...].astype(BF16))
        upd = jnp.zeros((GLA_DV, GLA_QK_W), F32)
        for h in range(GLA_HEADS):
            vcol = slice(2 * GLA_QK_W + h * GLA_DV, 2 * GLA_QK_W + (h + 1) * GLA_DV)
            gcol = slice(2 * GLA_QK_W + GLA_V_W + h * GLA_DV, 2 * GLA_QK_W + GLA_V_W + (h + 1) * GLA_DV)
            v_h = zg_ref[rows, vcol].astype(BF16)
            o = _dot(att[h * C:(h + 1) * C], jnp.concatenate([v_h] * n_sub, axis=0)) + o_inter[h * C:(h + 1) * C]
            gate = zg_ref[rows, gcol]
            a_ref[rows, h * GLA_DV:(h + 1) * GLA_DV] = (
                _rms(o, norm_ref[...]) * (gate * _sigmoid(gate))).astype(BF16)
            v_t = zg_ref[pair, vcol].T
            u_t = _dot(jnp.where(pair_chunk == (c % 2), v_t, 0.0).astype(BF16), k_state[pair])
            upd = upd + jnp.where(st_lane_head == h, u_t, 0.0)
        st_ref[...] = st_ref[...] * decay[c * C:c * C + 1, :] + upd

    @pl.when(i == pl.num_programs(1) - 1)
    def _():
        sout_ref[...] = st_ref[...].T.reshape(1, GLA_HEADS, GLA_DK, GLA_DV)


def _gla_prompt(zg, la, norm, batch, seq):
    nblk = seq // GLA_BLOCK
    cum = jnp.asarray(_cum_matrices(GLA_BLOCK, GLA_CHUNK, GLA_SUB), BF16)
    row = lambda b, i: (b * nblk + i, 0)
    return pl.pallas_call(
        _gla_prompt_body,
        grid=(batch, nblk),
        in_specs=[
            pl.BlockSpec((GLA_BLOCK, ZG_W), row),
            pl.BlockSpec((GLA_BLOCK, GLA_QK_W), row),
            pl.BlockSpec((1, GLA_DV), lambda b, i: (0, 0)),
            pl.BlockSpec(cum.shape, lambda b, i: (0, 0)),
        ],
        out_specs=[
            pl.BlockSpec((GLA_BLOCK, GLA_V_W), row),
            pl.BlockSpec((1, GLA_HEADS, GLA_DK, GLA_DV), lambda b, i: (b, 0, 0, 0)),
        ],
        out_shape=(jax.ShapeDtypeStruct((batch * seq, GLA_V_W), BF16),
                   jax.ShapeDtypeStruct((batch, GLA_HEADS, GLA_DK, GLA_DV), F32)),
        scratch_shapes=[pltpu.VMEM((GLA_DV, GLA_QK_W), F32)],
        compiler_params=pltpu.CompilerParams(dimension_semantics=("arbitrary", "arbitrary"),
                                             vmem_limit_bytes=VMEM_LIMIT),
        name="gla_prompt",
    )(zg, la, norm, cum)


GLA_S_ROWS = 128


def _gla_sample_body(n_new, zg_ref, la_ref, s_ref, norm_ref, cum_ref, a_ref, sout_ref,
                     qin_ref, khat_ref, dec_ref, oint_ref, vt_ref):
    rows = GLA_S_ROWS
    cums = _dot01(cum_ref[...], la_ref[...])
    b = cums[0:rows]
    bl = cums[rows:2 * rows]
    q = zg_ref[:, 0:GLA_QK_W] * GLA_DK ** -0.5
    k = zg_ref[:, GLA_QK_W:2 * GLA_QK_W]
    q_dec = q * jnp.exp(b)
    k_intra = (k * jnp.exp(jnp.minimum(-b, EXP_CLAMP))).astype(BF16)
    qin_ref[...] = q_dec
    khat_ref[...] = (k * jnp.exp(bl - b)).astype(BF16)
    dec_ref[...] = jnp.exp(bl)

    hr = GLA_HEADS * rows
    head_sel = _div(_iota((hr, GLA_QK_W), 1), GLA_DK) == _div(_iota((hr, GLA_QK_W), 0), rows)
    qst = jnp.where(head_sel, jnp.concatenate([q_dec] * GLA_HEADS, axis=0), 0.0).astype(BF16)
    tt = _mod(_iota((hr, rows), 0), rows)
    ss = _iota((hr, rows), 1)
    keep = (_div(tt, n_new) == _div(ss, n_new)) & (ss <= tt)
    att = jnp.where(keep, _dot_nt(qst, k_intra), 0.0).astype(BF16)
    for h in range(GLA_HEADS):
        vcol = slice(2 * GLA_QK_W + h * GLA_DV, 2 * GLA_QK_W + (h + 1) * GLA_DV)
        v_h = zg_ref[:, vcol]
        oint_ref[:, h * GLA_DV:(h + 1) * GLA_DV] = _dot(att[h * rows:(h + 1) * rows], v_h.astype(BF16))
        vt_ref[h] = v_h.T

    per8 = SUBLANES // n_new
    sub8 = _div(_iota((SUBLANES, GLA_QK_W), 0), n_new)
    lane8 = _div(_iota((SUBLANES, GLA_QK_W), 1), GLA_DK)
    out8 = _div(_iota((SUBLANES, GLA_DV), 0), n_new)
    vt_seq = _div(_iota((GLA_DV, rows), 1), n_new)
    st_lane_head = _div(_iota((GLA_DV, GLA_QK_W), 1), GLA_DK)

    def group_step(m, carry):
        r8 = pl.ds(pl.multiple_of(m * SUBLANES, SUBLANES), SUBLANES)
        q8 = qin_ref[r8, :]
        d8 = dec_ref[r8, :]
        res = []
        for e in range(per8):
            n = m * per8 + e
            s_n = s_ref[0, n].reshape(GLA_HEADS * GLA_DK, GLA_DV)
            lhs = jnp.concatenate([jnp.where((sub8 == e) & (lane8 == h), q8, 0.0) for h in range(GLA_HEADS)],
                                  axis=0)
            res.append(_dot(lhs.astype(BF16), s_n.astype(BF16)))
            upd = jnp.zeros((GLA_DV, GLA_QK_W), F32)
            for h in range(GLA_HEADS):
                u_t = _dot(jnp.where(vt_seq == n, vt_ref[h], 0.0).astype(BF16), khat_ref[...])
                upd = upd + jnp.where(st_lane_head == h, u_t, 0.0)
            s_new_t = s_n.T * d8[e * n_new:e * n_new + 1, :] + upd
            sout_ref[0, n] = s_new_t.T.reshape(GLA_HEADS, GLA_DK, GLA_DV)
        for h in range(GLA_HEADS):
            o8 = res[0][h * SUBLANES:(h + 1) * SUBLANES]
            for e in range(1, per8):
                o8 = jnp.where(out8 == e, res[e][h * SUBLANES:(h + 1) * SUBLANES], o8)
            oint_ref[r8, h * GLA_DV:(h + 1) * GLA_DV] += o8
        return carry

    lax.fori_loop(0, rows // SUBLANES, group_step, 0)

    for h in range(GLA_HEADS):
        gcol = slice(2 * GLA_QK_W + GLA_V_W + h * GLA_DV, 2 * GLA_QK_W + GLA_V_W + (h + 1) * GLA_DV)
        gate = zg_ref[:, gcol]
        o = oint_ref[:, h * GLA_DV:(h + 1) * GLA_DV]
        a_ref[:, h * GLA_DV:(h + 1) * GLA_DV] = (_rms(o, norm_ref[...]) * (gate * _sigmoid(gate))).astype(BF16)


def _gla_sample(zg, la, state, norm, row0, n_seq, n_new):
    assert SUBLANES % n_new == 0 and GLA_S_ROWS % n_new == 0
    seqs = GLA_S_ROWS // n_new
    steps = n_seq // seqs
    blk0 = row0 // GLA_S_ROWS
    cum = jnp.asarray(_cum_matrices(GLA_S_ROWS, n_new), BF16)
    return pl.pallas_call(
        functools.partial(_gla_sample_body, n_new),
        grid=(steps,),
        in_specs=[
            pl.BlockSpec((GLA_S_ROWS, ZG_W), lambda i: (blk0 + i, 0)),
            pl.BlockSpec((GLA_S_ROWS, GLA_QK_W), lambda i: (blk0 + i, 0)),
            pl.BlockSpec((1, seqs, GLA_HEADS, GLA_DK, GLA_DV), lambda i: (0, i, 0, 0, 0)),
            pl.BlockSpec((1, GLA_DV), lambda i: (0, 0)),
            pl.BlockSpec(cum.shape, lambda i: (0, 0)),
        ],
        out_specs=[
            pl.BlockSpec((GLA_S_ROWS, GLA_V_W), lambda i: (i, 0)),
            pl.BlockSpec((1, seqs, GLA_HEADS, GLA_DK, GLA_DV), lambda i: (0, i, 0, 0, 0)),
        ],
        out_shape=(jax.ShapeDtypeStruct((n_seq * n_new, GLA_V_W), BF16),
                   jax.ShapeDtypeStruct((1, n_seq, GLA_HEADS, GLA_DK, GLA_DV), F32)),
        scratch_shapes=[
            pltpu.VMEM((GLA_S_ROWS, GLA_QK_W), F32),
            pltpu.VMEM((GLA_S_ROWS, GLA_QK_W), BF16),
            pltpu.VMEM((GLA_S_ROWS, GLA_QK_W), F32),
            pltpu.VMEM((GLA_S_ROWS, GLA_V_W), F32),
            pltpu.VMEM((GLA_HEADS, GLA_DV, GLA_S_ROWS), F32),
        ],
        compiler_params=pltpu.CompilerParams(dimension_semantics=("arbitrary",),
                                             vmem_limit_bytes=VMEM_LIMIT),
        name="gla_sample",
    )(zg, la, state, norm, cum)


SB_TILE = 256
SB_PAIR = LANES // SB_DH


def _sb_prompt_body(bias_ref, q_ref, k_ref, v_ref, norm_ref, u_ref, o_ref, acc_ref, carry_ref):
    p = pl.program_id(1)
    qi = pl.program_id(2)
    T = SB_TILE
    lane_head = _div(_iota((T, LANES), 1), SB_DH)
    q = q_ref[...]
    qh = [jnp.where(lane_head == e, q, jnp.zeros_like(q)) for e in range(SB_PAIR)]
    bias = [bias_ref[p * SB_PAIR + e] for e in range(SB_PAIR)]
    u = u_ref[...]
    acc_ref[...] = jnp.zeros_like(acc_ref)
    valid = _iota((T, T), 1) < _iota((T, T), 0)

    def head_tile(e, kt, vt, masked):
        z = _dot_nt(qh[e], kt) + bias[e]
        sp = _softplus2(z)
        if masked:
            sp = jnp.where(valid, sp, 0.0)
        cs = _dot(sp.astype(BF16), u)
        w = jnp.exp2(z - sp - cs)
        if masked:
            w = jnp.where(valid, w, 0.0)
        return _dot(w.astype(BF16), vt), jnp.sum(sp, axis=1, keepdims=True)

    def tiles(kjs, masked=False):
        res = []
        for kj in kjs:
            rows = pl.ds(pl.multiple_of(kj * T, T), T)
            kt, vt = k_ref[rows, :], v_ref[rows, :]
            res.append([head_tile(e, kt, vt, masked) for e in range(SB_PAIR)])
        for per_head in res:
            for e, (r, tot) in enumerate(per_head):
                acc_ref[e] += jnp.exp2(-carry_ref[e]) * r
                carry_ref[e] += tot

    carry_ref[...] = jnp.zeros_like(carry_ref)
    tiles([qi], masked=True)
    n_full = qi

    @pl.when(n_full % 2 == 1)
    def _():
        tiles([qi - 1])

    first = qi - 1 - n_full % 2

    def pair_step(t, c):
        tiles([first - 2 * t, first - 2 * t - 1])
        return c

    lax.fori_loop(0, n_full // 2, pair_step, 0)

    o = acc_ref[0]
    for e in range(1, SB_PAIR):
        o = jnp.where(lane_head == e, acc_ref[e], o)
    sq = o * o
    ms = jnp.zeros_like(o)
    for e in range(SB_PAIR):
        ms_e = jnp.sum(jnp.where(lane_head == e, sq, 0.0), axis=1, keepdims=True) * (1.0 / SB_DH)
        ms = jnp.where(lane_head == e, ms_e, ms)
    o_ref[...] = (o * lax.rsqrt(ms + EPS) * norm_ref[...]).astype(BF16)


def _sb_prompt(bias, sq_b, sk_b, sv_b, norm_pair, batch, seq):
    nq = seq // SB_TILE
    npair = SB_HEADS // SB_PAIR
    j = np.arange(SB_TILE)[:, None]
    s = np.arange(SB_TILE)[None, :]
    u = jnp.asarray((j > s).astype(np.float32), BF16)
    return pl.pallas_call(
        _sb_prompt_body,
        grid=(batch, npair, nq),
        in_specs=[
            pl.BlockSpec(memory_space=pltpu.SMEM),
            pl.BlockSpec((SB_TILE, LANES), lambda b, p, i: (b * nq + i, p)),
            pl.BlockSpec((seq, LANES), lambda b, p, i: (b, p)),
            pl.BlockSpec((seq, LANES), lambda b, p, i: (b, p)),
            pl.BlockSpec((1, LANES), lambda b, p, i: (0, 0)),
            pl.BlockSpec((SB_TILE, SB_TILE), lambda b, p, i: (0, 0)),
        ],
        out_specs=pl.BlockSpec((SB_TILE, LANES), lambda b, p, i: (b * nq + i, p)),
        out_shape=jax.ShapeDtypeStruct((batch * seq, SB_W), BF16),
        scratch_shapes=[pltpu.VMEM((SB_PAIR, SB_TILE, LANES), F32), pltpu.VMEM((SB_PAIR, SB_TILE, 1), F32)],
        compiler_params=pltpu.CompilerParams(dimension_semantics=("arbitrary", "arbitrary", "arbitrary"),
                                             vmem_limit_bytes=VMEM_LIMIT),
        name="sb_prompt",
    )(bias, sq_b, sk_b, sv_b, norm_pair, u)


def _sb_scan_matrix():
    s_in = np.arange(LANES)[:, None]
    s_out = np.arange(LANES)[None, :]
    return np.concatenate([s_in > s_out, np.ones((LANES, LANES), bool)], axis=1).astype(np.float32)


def _sb_sample_body(n_new, n_pages, pt_ref, bias_ref, q_ref, kn_ref, vn_ref, *rest):
    del pt_ref
    k_refs, v_refs = rest[:n_pages], rest[n_pages:2 * n_pages]
    norm_ref, up_ref, o_ref = rest[2 * n_pages:]
    qrows = n_new * SB_HEADS
    own_head = _mod(_iota((qrows, SB_W), 0), SB_HEADS) == _div(_iota((qrows, SB_W), 1), SB_DH)
    q_rows = jnp.concatenate([jnp.broadcast_to(q_ref[0, t:t + 1, :], (SB_HEADS, SB_W)) for t in range(n_new)],
                             axis=0)
    qbd = jnp.where(own_head, q_rows, 0.0).astype(BF16)
    bias = bias_ref[...]

    zs = [_dot(qbd, k_refs[g][...].reshape(SB_W, PAGE_SIZE).astype(BF16)) + bias for g in range(n_pages)]
    pad = jnp.zeros((LANES - SUBLANES, SB_W), F32)
    kn = jnp.concatenate([kn_ref[0], pad], axis=0).astype(BF16)
    vn = jnp.concatenate([vn_ref[0], pad], axis=0).astype(BF16)
    zs.append(_dot_nt(qbd, kn) + bias)
    new_ok = _iota((qrows, LANES), 1) < _div(_iota((qrows, LANES), 0), SB_HEADS)
    sps = [_softplus2(z) for z in zs]
    sps[n_pages] = jnp.where(new_ok, sps[n_pages], 0.0)
    r = _dot(jnp.concatenate(sps, axis=0).astype(BF16), up_ref[...])
    run = jnp.zeros((qrows, LANES), F32)
    acc = jnp.zeros((qrows, SB_W), F32)
    for g in reversed(range(n_pages + 1)):
        blk = r[g * qrows:(g + 1) * qrows]
        w = jnp.exp2(zs[g] - sps[g] - blk[:, 0:LANES] - run)
        run = run + blk[:, LANES:2 * LANES]
        if g == n_pages:
            acc = acc + _dot(jnp.where(new_ok, w, 0.0).astype(BF16), vn)
        else:
            acc = acc + _dot_nt(w.astype(BF16), v_refs[g][...].reshape(SB_W, PAGE_SIZE).astype(BF16))
    acc = jnp.where(own_head, acc, 0.0)
    o = acc[:, 0:SB_DH]
    for h in range(1, SB_HEADS):
        o = o + acc[:, h * SB_DH:(h + 1) * SB_DH]
    o_ref[0] = _rms(o, norm_ref[...]).astype(BF16)


def _sb_sample(page_table, bias_rows, q_s, k_new, v_new, pool_kt, pool_vt, norm, layer, n_new):
    n_seq, n_pages = page_table.shape
    qrows = n_new * SB_HEADS
    assert PAGE_SIZE == LANES and n_new <= SUBLANES
    up = jnp.asarray(_sb_scan_matrix(), BF16)
    seq3 = lambda b, pt: (b, 0, 0)
    const = lambda b, pt: (0, 0)
    page_specs = [pl.BlockSpec((None, None, SB_HEADS, SB_DH, PAGE_SIZE),
                               lambda b, pt, g=g: (layer, pt[b * n_pages + g], 0, 0, 0)) for g in range(n_pages)]
    grid_spec = pltpu.PrefetchScalarGridSpec(
        num_scalar_prefetch=1,
        grid=(n_seq,),
        in_specs=[
            pl.BlockSpec(bias_rows.shape, const),
            pl.BlockSpec((1, SUBLANES, SB_W), seq3),
            pl.BlockSpec((1, SUBLANES, SB_W), seq3),
            pl.BlockSpec((1, SUBLANES, SB_W), seq3),
            *page_specs, *page_specs,
            pl.BlockSpec((1, SB_DH), const),
            pl.BlockSpec(up.shape, const),
        ],
        out_specs=pl.BlockSpec((1, qrows, SB_DH), seq3),
    )
    return pl.pallas_call(
        functools.partial(_sb_sample_body, n_new, n_pages),
        grid_spec=grid_spec,
        out_shape=jax.ShapeDtypeStruct((n_seq, qrows, SB_DH), BF16),
        compiler_params=pltpu.CompilerParams(dimension_semantics=("arbitrary",),
                                             vmem_limit_bytes=VMEM_LIMIT),
        name="sb_sample",
    )(page_table.reshape(-1), bias_rows, q_s, k_new, v_new, *([pool_kt] * n_pages), *([pool_vt] * n_pages),
      norm, up)


def kernel(x_prompt, x_sample, cache_sb_k, cache_sb_v, state_gla, page_table, p_prompt, p_sample, f1_pre, f1_in, f1_out, f1_post, mix_pre, w_in, gate_w, gate_b, gla_norm, sb_norm, sb_bias, w_out, mix_post, f2_pre, f2_in, f2_out, f2_post, ple_pre, ple_w_gate, ple_w_up, ple_post):
    batch, seq, _ = x_prompt.shape
    n_seq, n_new, _ = x_sample.shape
    depth = f1_in.shape[0]
    n_p, n_s = batch * seq, n_seq * n_new
    assert n_p % TOK_TILE == 0 and n_s % TOK_TILE == 0 and seq % SB_TILE == 0 and n_p % GLA_S_ROWS == 0

    x = jnp.concatenate([x_prompt.reshape(n_p, D_MODEL), x_sample.reshape(n_s, D_MODEL)], axis=0)
    vec = lambda a: a.reshape(1, -1)
    lr0 = ZG_W
    sb0 = ZG_W + GLA_RANK
    pk, pv, ps, sk_l, sv_l, ss = [], [], [], [], [], []
    for l in range(depth):
        wcat = jnp.concatenate(
            [w_in[l][:, :lr0], w_in[l][:, sb0:], w_in[l][:, lr0:sb0],
             jnp.zeros((D_MODEL, LANES - GLA_RANK), F32)], axis=1).astype(BF16)
        gw_pad = jnp.concatenate([gate_w[l], jnp.zeros((LANES - GLA_RANK, GLA_QK_W), F32)], axis=0)
        x1, zg, la, sq_b, sk, sv, sk_b, sv_b = _ffn_in(
            x, vec(f1_pre[l]), f1_in[l].astype(BF16), f1_out[l].astype(BF16), vec(f1_post[l]),
            vec(mix_pre[l]), wcat, gw_pad, vec(gate_b[l]))

        a_p, s_gla_p = _gla_prompt(zg, la, vec(gla_norm[l]), batch, seq)
        norm_pair = vec(jnp.tile(sb_norm[l], SB_PAIR))
        bias2 = sb_bias[l] * LOG2E
        o_sb_p = _sb_prompt(bias2, sq_b, sk_b, sv_b, norm_pair, batch, seq)

        a_s, s_gla_s = _gla_sample(zg, la, state_gla[l:l + 1], vec(gla_norm[l]), n_p, n_seq, n_new)
        rows8 = lambda a: jnp.pad(a[n_p:].astype(F32).reshape(n_seq, n_new, SB_W),
                                  ((0, 0), (0, SUBLANES - n_new), (0, 0)))
        bias_rows = jnp.broadcast_to(jnp.tile(bias2, n_new)[:, None], (n_new * SB_HEADS, LANES))
        to_hdk = lambda c: jnp.transpose(c, (0, 1, 3, 4, 2))
        o_sb_s = _sb_sample(page_table, bias_rows, rows8(sq_b), rows8(sk), rows8(sv),
                            to_hdk(cache_sb_k), to_hdk(cache_sb_v), vec(sb_norm[l]), l, n_new)

        a = jnp.concatenate([a_p, a_s], axis=0)
        s = jnp.concatenate([o_sb_p, o_sb_s.reshape(n_s, SB_W)], axis=0)
        p = jnp.concatenate([p_prompt[l].reshape(n_p, PLE_DIM), p_sample[l].reshape(n_s, PLE_DIM)], axis=0)
        x = _merge_out(x1, a, s, p, w_out[l].astype(BF16), vec(mix_post[l]), vec(f2_pre[l]),
                       f2_in[l].astype(BF16), f2_out[l].astype(BF16), vec(f2_post[l]), vec(ple_pre[l]),
                       ple_w_gate[l].astype(BF16), ple_w_up[l].astype(BF16), vec(ple_post[l]))

        pk.append(sk[:n_p].reshape(batch, seq, SB_HEADS, SB_DH))
        pv.append(sv[:n_p].reshape(batch, seq, SB_HEADS, SB_DH))
        ps.append(s_gla_p)
        sk_l.append(sk[n_p:].reshape(n_seq, n_new, SB_HEADS, SB_DH))
        sv_l.append(sv[n_p:].reshape(n_seq, n_new, SB_HEADS, SB_DH))
        ss.append(s_gla_s[0])
    return (x[:n_p].reshape(batch, seq, D_MODEL), x[n_p:].reshape(n_seq, n_new, D_MODEL),
            jnp.stack(pk), jnp.stack(pv), jnp.stack(ps), jnp.stack(sk_l), jnp.stack(sv_l), jnp.stack(ss))
```

```python
import functools

import numpy as np
import jax
import jax.numpy as jnp
from jax import lax
from jax.experimental import pallas as pl
from jax.experimental.pallas import tpu as pltpu

F32 = jnp.float32
BF16 = jnp.bfloat16

D_MODEL = 1024
D_FF = 2816
PLE_DIM = 256
GLA_HEADS = 4
GLA_DK = 64
GLA_DV = 128
GLA_RANK = 16
GLA_GATE_TEMP = 16.0
GLA_CHUNK = 64
GLA_SUB = 16
SB_HEADS = 8
SB_DH = 64
PAGE_SIZE = 128
EPS = 1e-6
GLA_QK_W = GLA_HEADS * GLA_DK
GLA_V_W = GLA_HEADS * GLA_DV
SB_W = SB_HEADS * SB_DH
ZG_W = 2 * GLA_QK_W + 2 * GLA_V_W
LANES = 128
SUBLANES = 8
EXP_CLAMP = 80.0
LOG2E = 1.4426950408889634

TOK_TILE = 512
FF_TILE = 256
VMEM_LIMIT = 60 * 1024 * 1024


def _log2(n):
    k = int(n).bit_length() - 1
    assert 1 << k == n, n
    return k


def _div(x, n):
    return lax.shift_right_logical(x, _log2(n))


def _mod(x, n):
    assert n & (n - 1) == 0
    return x & (n - 1)


def _iota(shape, dim):
    return lax.broadcasted_iota(jnp.int32, shape, dim)


def _dot(a, b):
    return jnp.dot(a, b, preferred_element_type=F32)


def _dot_nt(a, b):
    return lax.dot_general(a, b, (((1,), (1,)), ((), ())), preferred_element_type=F32)


def _dot01(t_bf16, x):
    hi = x.astype(BF16)
    lo = (x - hi.astype(F32)).astype(BF16)
    return _dot(t_bf16, hi) + _dot(t_bf16, lo)


def _rms(x, w):
    ms = jnp.mean(x * x, axis=-1, keepdims=True)
    return x * lax.rsqrt(ms + EPS) * w


def _sigmoid(x):
    return 1.0 / (1.0 + jnp.exp(-x))


def _softplus2(x):
    return jnp.maximum(x, jnp.log(1.0 + jnp.exp2(jnp.minimum(x, 126.0))) * LOG2E)


def _swiglu_ffn(h, win_ref, wout_ref, acc_ref):
    for j in range(D_FF // FF_TILE):
        g = _dot(h, win_ref[:, j * FF_TILE:(j + 1) * FF_TILE])
        u = _dot(h, win_ref[:, D_FF + j * FF_TILE:D_FF + (j + 1) * FF_TILE])
        act = (g * _sigmoid(g) * u).astype(BF16)
        part = _dot(act, wout_ref[j * FF_TILE:(j + 1) * FF_TILE, :])
        if j == 0:
            acc_ref[...] = part
        else:
            acc_ref[...] += part


def _resident(shape):
    return pl.BlockSpec(shape, lambda i: (0,) * len(shape), pipeline_mode=pl.Buffered(1))


def _ffn_in_body(n_ptiles, xp_ref, xs_ref, pre_ref, win_ref, wout_ref, post_ref, mixpre_ref, wcat_ref, gw_ref,
                 gb_ref, x1_ref, zg_ref, la_ref, sq_ref, skb_ref, svb_ref, skt_ref, svt_ref, sks_ref, svs_ref,
                 acc_ref):
    is_prompt = pl.program_id(0) < n_ptiles
    x = jnp.where(is_prompt, xp_ref[...], xs_ref[...])
    _swiglu_ffn(_rms(x, pre_ref[...]).astype(BF16), win_ref, wout_ref, acc_ref)
    x1 = x + 0.5 * _rms(acc_ref[...], post_ref[...])
    x1_ref[...] = x1
    h2 = _rms(x1, mixpre_ref[...]).astype(BF16)
    zg_ref[...] = _dot(h2, wcat_ref[:, 0:ZG_W])
    sb = _dot(h2, wcat_ref[:, ZG_W:ZG_W + 3 * SB_W])
    sq_ref[...] = (sb[:, 0:SB_W] * (SB_DH ** -0.5 * LOG2E)).astype(BF16)
    sk = sb[:, SB_W:2 * SB_W]
    sv = sb[:, 2 * SB_W:3 * SB_W]
    skb_ref[...] = sk.astype(BF16)
    svb_ref[...] = sv.astype(BF16)

    @pl.when(is_prompt)
    def _():
        skt_ref[0] = sk.T.reshape(SB_HEADS, SB_DH, TOK_TILE)
        svt_ref[0] = sv.T.reshape(SB_HEADS, SB_DH, TOK_TILE)

    @pl.when(jnp.logical_not(is_prompt))
    def _():
        sks_ref[...] = sk
        svs_ref[...] = sv

    lr = _dot(h2, wcat_ref[:, ZG_W + 3 * SB_W:])
    logit = jnp.dot(lr, gw_ref[...], precision=lax.Precision.HIGHEST,
                    preferred_element_type=F32) + gb_ref[...]
    la_ref[...] = (jnp.minimum(logit, 0.0) - jnp.log1p(jnp.exp(-jnp.abs(logit)))) * (1.0 / GLA_GATE_TEMP)


def _ffn_in(x_p, x_s, batch, pre, w_in_b, w_out_b, post, mixpre, wcat_b, gw_pad, gb):
    n_p, n_s = x_p.shape[0], x_s.shape[0]
    n = n_p + n_s
    seq = n_p // batch
    n_pt, per_seq = n_p // TOK_TILE, seq // TOK_TILE
    row = lambda i: (i, 0)
    p_row = lambda i: (jnp.minimum(i, n_pt - 1), 0)
    s_row = lambda i: (jnp.maximum(i - n_pt, 0), 0)
    p_kv = lambda i: (jnp.minimum(i, n_pt - 1) // per_seq, 0, 0, jnp.minimum(i, n_pt - 1) % per_seq)
    vecd = _resident((1, D_MODEL))
    out_shapes = (
        jax.ShapeDtypeStruct((n, D_MODEL), F32),
        jax.ShapeDtypeStruct((n, ZG_W), F32),
        jax.ShapeDtypeStruct((n, GLA_QK_W), F32),
        jax.ShapeDtypeStruct((n, SB_W), BF16),
        jax.ShapeDtypeStruct((n, SB_W), BF16),
        jax.ShapeDtypeStruct((n, SB_W), BF16),
        jax.ShapeDtypeStruct((batch, SB_HEADS, SB_DH, seq), F32),
        jax.ShapeDtypeStruct((batch, SB_HEADS, SB_DH, seq), F32),
        jax.ShapeDtypeStruct((n_s, SB_W), F32),
        jax.ShapeDtypeStruct((n_s, SB_W), F32),
    )
    return pl.pallas_call(
        functools.partial(_ffn_in_body, n_pt),
        grid=(n // TOK_TILE,),
        in_specs=[
            pl.BlockSpec((TOK_TILE, D_MODEL), p_row),
            pl.BlockSpec((TOK_TILE, D_MODEL), s_row),
            vecd,
            _resident(w_in_b.shape),
            _resident(w_out_b.shape),
            vecd, vecd,
            _resident(wcat_b.shape),
            _resident(gw_pad.shape),
            _resident((1, GLA_QK_W)),
        ],
        out_specs=[
            pl.BlockSpec((TOK_TILE, D_MODEL), row),
            pl.BlockSpec((TOK_TILE, ZG_W), row),
            pl.BlockSpec((TOK_TILE, GLA_QK_W), row),
            pl.BlockSpec((TOK_TILE, SB_W), row),
            pl.BlockSpec((TOK_TILE, SB_W), row),
            pl.BlockSpec((TOK_TILE, SB_W), row),
            pl.BlockSpec((1, SB_HEADS, SB_DH, TOK_TILE), p_kv),
            pl.BlockSpec((1, SB_HEADS, SB_DH, TOK_TILE), p_kv),
            pl.BlockSpec((TOK_TILE, SB_W), s_row),
            pl.BlockSpec((TOK_TILE, SB_W), s_row),
        ],
        out_shape=out_shapes,
        scratch_shapes=[pltpu.VMEM((TOK_TILE, D_MODEL), F32)],
        compiler_params=pltpu.CompilerParams(dimension_semantics=("arbitrary",),
                                             vmem_limit_bytes=VMEM_LIMIT),
        name="ffn_in",
    )(x_p, x_s, pre, w_in_b, w_out_b, post, mixpre, wcat_b, gw_pad, gb)


def _merge_out_body(n_ptiles, x1_ref, ap_ref, as_ref, sp_ref, ss_ref, pp_ref, ps_ref, wout_ref, mixpost_ref,
                    pre_ref, win_ref, woutf_ref, post_ref, plepre_ref, pleg_ref, pleu_ref, plepost_ref,
                    yp_ref, ys_ref, acc_ref):
    is_prompt = pl.program_id(0) < n_ptiles
    a = jnp.where(is_prompt, ap_ref[...], as_ref[...])
    s = jnp.where(is_prompt, sp_ref[...], ss_ref[...])
    p = jnp.where(is_prompt, pp_ref[...], ps_ref[...])
    m = _dot(a, wout_ref[0:GLA_V_W, :]) + _dot(s, wout_ref[GLA_V_W:, :])
    x2 = x1_ref[...] + _rms(m, mixpost_ref[...])
    _swiglu_ffn(_rms(x2, pre_ref[...]).astype(BF16), win_ref, woutf_ref, acc_ref)
    x3 = x2 + 0.5 * _rms(acc_ref[...], post_ref[...])
    hg = _rms(x3, plepre_ref[...]).astype(BF16)
    gate = _sigmoid(_dot(hg, pleg_ref[...]))
    e = _dot(p.astype(BF16), pleu_ref[...])
    y = x3 + _rms(gate * e, plepost_ref[...])

    @pl.when(is_prompt)
    def _():
        yp_ref[...] = y

    @pl.when(jnp.logical_not(is_prompt))
    def _():
        ys_ref[...] = y


def _merge_out(x1, a_p, a_s, s_p, s_s, p_p, p_s, wout_b, mixpost, pre, w_in_b, w_out_b, post, plepre, pleg_b,
               pleu_b, plepost):
    n_p, n_s = a_p.shape[0], a_s.shape[0]
    n_pt = n_p // TOK_TILE
    row = lambda i: (i, 0)
    p_row = lambda i: (jnp.minimum(i, n_pt - 1), 0)
    s_row = lambda i: (jnp.maximum(i - n_pt, 0), 0)
    pair = lambda width: [pl.BlockSpec((TOK_TILE, width), p_row), pl.BlockSpec((TOK_TILE, width), s_row)]
    vecd = _resident((1, D_MODEL))
    return pl.pallas_call(
        functools.partial(_merge_out_body, n_pt),
        grid=((n_p + n_s) // TOK_TILE,),
        in_specs=[
            pl.BlockSpec((TOK_TILE, D_MODEL), row),
            *pair(GLA_V_W), *pair(SB_W), *pair(PLE_DIM),
            _resident(wout_b.shape),
            vecd, vecd,
            _resident(w_in_b.shape),
            _resident(w_out_b.shape),
            vecd, vecd,
            _resident(pleg_b.shape),
            _resident(pleu_b.shape),
            vecd,
        ],
        out_specs=pair(D_MODEL),
        out_shape=(jax.ShapeDtypeStruct((n_p, D_MODEL), F32), jax.ShapeDtypeStruct((n_s, D_MODEL), F32)),
        scratch_shapes=[pltpu.VMEM((TOK_TILE, D_MODEL), F32)],
        compiler_params=pltpu.CompilerParams(dimension_semantics=("arbitrary",),
                                             vmem_limit_bytes=VMEM_LIMIT),
        name="merge_out",
    )(x1, a_p, a_s, s_p, s_s, p_p, p_s, wout_b, mixpost, pre, w_in_b, w_out_b, post, plepre, pleg_b, pleu_b,
      plepost)


GLA_BLOCK = 256
GLA_PAIR = 128


def _cum_matrices(block, chunk, sub=None):
    t = np.arange(block)[:, None]
    s = np.arange(block)[None, :]
    same = (t // chunk) == (s // chunk)
    mats = [same & (s <= t)]
    if sub is not None:
        mats.append(same & ((s % chunk) < sub * ((t % chunk) // sub)))
    mats.append(same)
    return np.concatenate(mats, axis=0).astype(np.float32)


def _gla_prompt_body(zg_ref, la_ref, norm_ref, cum_ref, a_ref, sout_ref, st_ref):
    i = pl.program_id(1)

    @pl.when(i == 0)
    def _():
        st_ref[...] = jnp.zeros_like(st_ref)

    nb = GLA_BLOCK
    cums = _dot01(cum_ref[...], la_ref[...])
    b = cums[0:nb]
    r = cums[nb:2 * nb]
    bl = cums[2 * nb:3 * nb]
    q = zg_ref[:, 0:GLA_QK_W] * GLA_DK ** -0.5
    k = zg_ref[:, GLA_QK_W:2 * GLA_QK_W]
    q_intra = q * jnp.exp(b - r)
    q_inter = q * jnp.exp(b)
    k_state = (k * jnp.exp(bl - b)).astype(BF16)
    decay = jnp.exp(bl)

    C, S = GLA_CHUNK, GLA_SUB
    n_sub = C // S
    hc = GLA_HEADS * C
    head_sel = _div(_iota((hc, GLA_QK_W), 1), GLA_DK) == _div(_iota((hc, GLA_QK_W), 0), C)
    tt = _mod(_iota((hc, n_sub * C), 0), C)
    cc = _iota((hc, n_sub * C), 1)
    att_keep = (_div(cc, C) == _div(tt, S)) & (_mod(cc, C) <= tt)
    srow = _iota((C, GLA_QK_W), 0)
    pair_chunk = _div(_iota((GLA_DV, GLA_PAIR), 1), C)
    st_lane_head = _div(_iota((GLA_DV, GLA_QK_W), 1), GLA_DK)

    for c in range(nb // C):
        rows = slice(c * C, (c + 1) * C)
        pair = slice((c // 2) * GLA_PAIR, (c // 2 + 1) * GLA_PAIR)
        b_c, k_c = b[rows], k[rows]
        kst = []
        for sidx in range(n_sub):
            r_row = r[c * C + sidx * S:c * C + sidx * S + 1, :]
            kk = k_c * jnp.exp(jnp.minimum(r_row - b_c, EXP_CLAMP))
            kst.append(jnp.where(srow < (sidx + 1) * S, kk, 0.0))
        kst = jnp.concatenate(kst, axis=0).astype(BF16)
        qst = jnp.where(head_sel, jnp.concatenate([q_intra[rows]] * GLA_HEADS, axis=0), 0.0).astype(BF16)
        att = jnp.where(att_keep, _dot_nt(qst, kst), 0.0).astype(BF16)
        qin = jnp.where(head_sel, jnp.concatenate([q_inter[rows]] * GLA_HEADS, axis=0), 0.0).astype(BF16)
        o_inter = _dot_nt(qin, st_ref[...].astype(BF16))
        upd = jnp.zeros((GLA_DV, GLA_QK_W), F32)
        for h in range(GLA_HEADS):
            vcol = slice(2 * GLA_QK_W + h * GLA_DV, 2 * GLA_QK_W + (h + 1) * GLA_DV)
            gcol = slice(2 * GLA_QK_W + GLA_V_W + h * GLA_DV, 2 * GLA_QK_W + GLA_V_W + (h + 1) * GLA_DV)
            v_h = zg_ref[rows, vcol].astype(BF16)
            o = _dot(att[h * C:(h + 1) * C], jnp.concatenate([v_h] * n_sub, axis=0)) + o_inter[h * C:(h + 1) * C]
            gate = zg_ref[rows, gcol]
            a_ref[rows, h * GLA_DV:(h + 1) * GLA_DV] = (
                _rms(o, norm_ref[...]) * (gate * _sigmoid(gate))).astype(BF16)
            v_t = zg_ref[pair, vcol].T
            u_t = _dot(jnp.where(pair_chunk == (c % 2), v_t, 0.0).astype(BF16), k_state[pair])
            upd = upd + jnp.where(st_lane_head == h, u_t, 0.0)
        st_ref[...] = st_ref[...] * decay[c * C:c * C + 1, :] + upd

    @pl.when(i == pl.num_programs(1) - 1)
    def _():
        sout_ref[...] = st_ref[...].T.reshape(1, GLA_HEADS, GLA_DK, GLA_DV)


def _gla_prompt(zg, la, norm, batch, seq):
    nblk = seq // GLA_BLOCK
    cum = jnp.asarray(_cum_matrices(GLA_BLOCK, GLA_CHUNK, GLA_SUB), BF16)
    row = lambda b, i: (b * nblk + i, 0)
    return pl.pallas_call(
        _gla_prompt_body,
        grid=(batch, nblk),
        in_specs=[
            pl.BlockSpec((GLA_BLOCK, ZG_W), row),
            pl.BlockSpec((GLA_BLOCK, GLA_QK_W), row),
            pl.BlockSpec((1, GLA_DV), lambda b, i: (0, 0)),
            pl.BlockSpec(cum.shape, lambda b, i: (0, 0)),
        ],
        out_specs=[
            pl.BlockSpec((GLA_BLOCK, GLA_V_W), row),
            pl.BlockSpec((1, GLA_HEADS, GLA_DK, GLA_DV), lambda b, i: (b, 0, 0, 0)),
        ],
        out_shape=(jax.ShapeDtypeStruct((batch * seq, GLA_V_W), BF16),
                   jax.ShapeDtypeStruct((batch, GLA_HEADS, GLA_DK, GLA_DV), F32)),
        scratch_shapes=[pltpu.VMEM((GLA_DV, GLA_QK_W), F32)],
        compiler_params=pltpu.CompilerParams(dimension_semantics=("arbitrary", "arbitrary"),
                                             vmem_limit_bytes=VMEM_LIMIT),
        name="gla_prompt",
    )(zg, la, norm, cum)


GLA_S_ROWS = 128


def _gla_sample_body(n_new, zg_ref, la_ref, s_ref, norm_ref, cum_ref, a_ref, sout_ref,
                     qin_ref, khat_ref, dec_ref, oint_ref, vt_ref):
    rows = GLA_S_ROWS
    cums = _dot01(cum_ref[...], la_ref[...])
    b = cums[0:rows]
    bl = cums[rows:2 * rows]
    q = zg_ref[:, 0:GLA_QK_W] * GLA_DK ** -0.5
    k = zg_ref[:, GLA_QK_W:2 * GLA_QK_W]
    q_dec = q * jnp.exp(b)
    k_intra = (k * jnp.exp(jnp.minimum(-b, EXP_CLAMP))).astype(BF16)
    qin_ref[...] = q_dec
    khat_ref[...] = (k * jnp.exp(bl - b)).astype(BF16)
    dec_ref[...] = jnp.exp(bl)

    hr = GLA_HEADS * rows
    head_sel = _div(_iota((hr, GLA_QK_W), 1), GLA_DK) == _div(_iota((hr, GLA_QK_W), 0), rows)
    qst = jnp.where(head_sel, jnp.concatenate([q_dec] * GLA_HEADS, axis=0), 0.0).astype(BF16)
    tt = _mod(_iota((hr, rows), 0), rows)
    ss = _iota((hr, rows), 1)
    keep = (_div(tt, n_new) == _div(ss, n_new)) & (ss <= tt)
    att = jnp.where(keep, _dot_nt(qst, k_intra), 0.0).astype(BF16)
    for h in range(GLA_HEADS):
        vcol = slice(2 * GLA_QK_W + h * GLA_DV, 2 * GLA_QK_W + (h + 1) * GLA_DV)
        v_h = zg_ref[:, vcol]
        oint_ref[:, h * GLA_DV:(h + 1) * GLA_DV] = _dot(att[h * rows:(h + 1) * rows], v_h.astype(BF16))
        vt_ref[h] = v_h.T

    per8 = SUBLANES // n_new
    sub8 = _div(_iota((SUBLANES, GLA_QK_W), 0), n_new)
    lane8 = _div(_iota((SUBLANES, GLA_QK_W), 1), GLA_DK)
    out8 = _div(_iota((SUBLANES, GLA_DV), 0), n_new)
    vt_seq = _div(_iota((GLA_DV, rows), 1), n_new)
    st_lane_head = _div(_iota((GLA_DV, GLA_QK_W), 1), GLA_DK)

    def group_step(m, carry):
        r8 = pl.ds(pl.multiple_of(m * SUBLANES, SUBLANES), SUBLANES)
        q8 = qin_ref[r8, :]
        d8 = dec_ref[r8, :]
        res = []
        for e in range(per8):
            n = m * per8 + e
            s_n = s_ref[0, n].reshape(GLA_HEADS * GLA_DK, GLA_DV)
            lhs = jnp.concatenate([jnp.where((sub8 == e) & (lane8 == h), q8, 0.0) for h in range(GLA_HEADS)],
                                  axis=0)
            res.append(_dot(lhs.astype(BF16), s_n.astype(BF16)))
            upd = jnp.zeros((GLA_DV, GLA_QK_W), F32)
            for h in range(GLA_HEADS):
                u_t = _dot(jnp.where(vt_seq == n, vt_ref[h], 0.0).astype(BF16), khat_ref[...])
                upd = upd + jnp.where(st_lane_head == h, u_t, 0.0)
            s_new_t = s_n.T * d8[e * n_new:e * n_new + 1, :] + upd
            sout_ref[0, n] = s_new_t.T.reshape(GLA_HEADS, GLA_DK, GLA_DV)
        for h in range(GLA_HEADS):
            o8 = res[0][h * SUBLANES:(h + 1) * SUBLANES]
            for e in range(1, per8):
                o8 = jnp.where(out8 == e, res[e][h * SUBLANES:(h + 1) * SUBLANES], o8)
            oint_ref[r8, h * GLA_DV:(h + 1) * GLA_DV] += o8
        return carry

    lax.fori_loop(0, rows // SUBLANES, group_step, 0)

    for h in range(GLA_HEADS):
        gcol = slice(2 * GLA_QK_W + GLA_V_W + h * GLA_DV, 2 * GLA_QK_W + GLA_V_W + (h + 1) * GLA_DV)
        gate = zg_ref[:, gcol]
        o = oint_ref[:, h * GLA_DV:(h + 1) * GLA_DV]
        a_ref[:, h * GLA_DV:(h + 1) * GLA_DV] = (_rms(o, norm_ref[...]) * (gate * _sigmoid(gate))).astype(BF16)


def _gla_sample(zg, la, state, norm, row0, n_seq, n_new):
    assert SUBLANES % n_new == 0 and GLA_S_ROWS % n_new == 0
    seqs = GLA_S_ROWS // n_new
    steps = n_seq // seqs
    blk0 = row0 // GLA_S_ROWS
    cum = jnp.asarray(_cum_matrices(GLA_S_ROWS, n_new), BF16)
    return pl.pallas_call(
        functools.partial(_gla_sample_body, n_new),
        grid=(steps,),
        in_specs=[
            pl.BlockSpec((GLA_S_ROWS, ZG_W), lambda i: (blk0 + i, 0)),
            pl.BlockSpec((GLA_S_ROWS, GLA_QK_W), lambda i: (blk0 + i, 0)),
            pl.BlockSpec((1, seqs, GLA_HEADS, GLA_DK, GLA_DV), lambda i: (0, i, 0, 0, 0)),
            pl.BlockSpec((1, GLA_DV), lambda i: (0, 0)),
            pl.BlockSpec(cum.shape, lambda i: (0, 0)),
        ],
        out_specs=[
            pl.BlockSpec((GLA_S_ROWS, GLA_V_W), lambda i: (i, 0)),
            pl.BlockSpec((1, seqs, GLA_HEADS, GLA_DK, GLA_DV), lambda i: (0, i, 0, 0, 0)),
        ],
        out_shape=(jax.ShapeDtypeStruct((n_seq * n_new, GLA_V_W), BF16),
                   jax.ShapeDtypeStruct((1, n_seq, GLA_HEADS, GLA_DK, GLA_DV), F32)),
        scratch_shapes=[
            pltpu.VMEM((GLA_S_ROWS, GLA_QK_W), F32),
            pltpu.VMEM((GLA_S_ROWS, GLA_QK_W), BF16),
            pltpu.VMEM((GLA_S_ROWS, GLA_QK_W), F32),
            pltpu.VMEM((GLA_S_ROWS, GLA_V_W), F32),
            pltpu.VMEM((GLA_HEADS, GLA_DV, GLA_S_ROWS), F32),
        ],
        compiler_params=pltpu.CompilerParams(dimension_semantics=("arbitrary",),
                                             vmem_limit_bytes=VMEM_LIMIT),
        name="gla_sample",
    )(zg, la, state, norm, cum)


SB_TILE = 256
SB_PAIR = LANES // SB_DH
SB_GROUP = 4


def _sb_prompt_body(bias_ref, q_ref, k_ref, v_ref, norm_ref, u_ref, o_ref, acc_ref, carry_ref):
    p = pl.program_id(1)
    qi = pl.program_id(2)
    T = SB_TILE
    lane_head = _div(_iota((T, LANES), 1), SB_DH)
    q = q_ref[...]
    qh = [jnp.where(lane_head == e, q, jnp.zeros_like(q)) for e in range(SB_PAIR)]
    bias = [bias_ref[p * SB_PAIR + e] for e in range(SB_PAIR)]
    u = u_ref[...]
    acc_ref[...] = jnp.zeros_like(acc_ref)
    valid = _iota((T, T), 1) < _iota((T, T), 0)

    def head_tile(e, kt, vt, masked):
        z = _dot_nt(qh[e], kt) + bias[e]
        sp = _softplus2(z)
        if masked:
            sp = jnp.where(valid, sp, 0.0)
        cs = _dot(sp.astype(BF16), u)
        w = jnp.exp2(z - sp - cs)
        if masked:
            w = jnp.where(valid, w, 0.0)
        return _dot(w.astype(BF16), vt), cs[:, 0:1] + sp[:, 0:1]

    def tiles(kjs, n_masked=0):
        res = []
        for n, kj in enumerate(kjs):
            rows = pl.ds(pl.multiple_of(kj * T, T), T)
            kt, vt = k_ref[rows, :], v_ref[rows, :]
            res.append([head_tile(e, kt, vt, n < n_masked) for e in range(SB_PAIR)])
        for per_head in res:
            for e, (r, tot) in enumerate(per_head):
                acc_ref[e] += jnp.exp2(-carry_ref[e]) * r
                carry_ref[e] += tot

    carry_ref[...] = jnp.zeros_like(carry_ref)
    n_rem = _mod(qi, SB_GROUP)
    for rem in range(SB_GROUP):
        @pl.when(n_rem == rem)
        def _(rem=rem):
            tiles([qi - g for g in range(rem + 1)], n_masked=1)

    first = qi - 1 - n_rem

    def group_step(t, c):
        tiles([first - SB_GROUP * t - g for g in range(SB_GROUP)])
        return c

    lax.fori_loop(0, _div(qi, SB_GROUP), group_step, 0)

    o = acc_ref[0]
    for e in range(1, SB_PAIR):
        o = jnp.where(lane_head == e, acc_ref[e], o)
    sq = o * o
    ms = jnp.zeros_like(o)
    for e in range(SB_PAIR):
        ms_e = jnp.sum(jnp.where(lane_head == e, sq, 0.0), axis=1, keepdims=True) * (1.0 / SB_DH)
        ms = jnp.where(lane_head == e, ms_e, ms)
    o_ref[...] = (o * lax.rsqrt(ms + EPS) * norm_ref[...]).astype(BF16)


def _sb_prompt(bias, sq_b, sk_b, sv_b, norm_pair, batch, seq):
    nq = seq // SB_TILE
    npair = SB_HEADS // SB_PAIR
    j = np.arange(SB_TILE)[:, None]
    s = np.arange(SB_TILE)[None, :]
    u = jnp.asarray((j > s).astype(np.float32), BF16)
    return pl.pallas_call(
        _sb_prompt_body,
        grid=(batch, npair, nq),
        in_specs=[
            pl.BlockSpec(memory_space=pltpu.SMEM),
            pl.BlockSpec((SB_TILE, LANES), lambda b, p, i: (b * nq + i, p)),
            pl.BlockSpec((seq, LANES), lambda b, p, i: (b, p)),
            pl.BlockSpec((seq, LANES), lambda b, p, i: (b, p)),
            pl.BlockSpec((1, LANES), lambda b, p, i: (0, 0)),
            pl.BlockSpec((SB_TILE, SB_TILE), lambda b, p, i: (0, 0)),
        ],
        out_specs=pl.BlockSpec((SB_TILE, LANES), lambda b, p, i: (b * nq + i, p)),
        out_shape=jax.ShapeDtypeStruct((batch * seq, SB_W), BF16),
        scratch_shapes=[pltpu.VMEM((SB_PAIR, SB_TILE, LANES), F32), pltpu.VMEM((SB_PAIR, SB_TILE, 1), F32)],
        compiler_params=pltpu.CompilerParams(dimension_semantics=("arbitrary", "arbitrary", "arbitrary"),
                                             vmem_limit_bytes=VMEM_LIMIT),
        name="sb_prompt",
    )(bias, sq_b, sk_b, sv_b, norm_pair, u)


def _sb_scan_matrix():
    s_in = np.arange(LANES)[:, None]
    s_out = np.arange(LANES)[None, :]
    return np.concatenate([s_in > s_out, np.ones((LANES, LANES), bool)], axis=1).astype(np.float32)


def _sb_sample_body(n_new, n_pages, pt_ref, bias_ref, q_ref, kn_ref, vn_ref, *rest):
    del pt_ref
    k_refs, v_refs = rest[:n_pages], rest[n_pages:2 * n_pages]
    norm_ref, up_ref, o_ref = rest[2 * n_pages:]
    qrows = n_new * SB_HEADS
    own_head = _mod(_iota((qrows, SB_W), 0), SB_HEADS) == _div(_iota((qrows, SB_W), 1), SB_DH)
    q_rows = jnp.concatenate([jnp.broadcast_to(q_ref[0, t:t + 1, :], (SB_HEADS, SB_W)) for t in range(n_new)],
                             axis=0)
    qbd = jnp.where(own_head, q_rows, 0.0).astype(BF16)
    bias = bias_ref[...]

    zs = [_dot(qbd, k_refs[g][...].reshape(SB_W, PAGE_SIZE).astype(BF16)) + bias for g in range(n_pages)]
    pad = jnp.zeros((LANES - SUBLANES, SB_W), F32)
    kn = jnp.concatenate([kn_ref[0], pad], axis=0).astype(BF16)
    vn = jnp.concatenate([vn_ref[0], pad], axis=0).astype(BF16)
    zs.append(_dot_nt(qbd, kn) + bias)
    new_ok = _iota((qrows, LANES), 1) < _div(_iota((qrows, LANES), 0), SB_HEADS)
    sps = [_softplus2(z) for z in zs]
    sps[n_pages] = jnp.where(new_ok, sps[n_pages], 0.0)
    r = _dot(jnp.concatenate(sps, axis=0).astype(BF16), up_ref[...])
    run = jnp.zeros((qrows, LANES), F32)
    acc = jnp.zeros((qrows, SB_W), F32)
    for g in reversed(range(n_pages + 1)):
        blk = r[g * qrows:(g + 1) * qrows]
        w = jnp.exp2(zs[g] - sps[g] - blk[:, 0:LANES] - run)
        run = run + blk[:, LANES:2 * LANES]
        if g == n_pages:
            acc = acc + _dot(jnp.where(new_ok, w, 0.0).astype(BF16), vn)
        else:
            acc = acc + _dot_nt(w.astype(BF16), v_refs[g][...].reshape(SB_W, PAGE_SIZE).astype(BF16))
    acc = jnp.where(own_head, acc, 0.0)
    o = acc[:, 0:SB_DH]
    for h in range(1, SB_HEADS):
        o = o + acc[:, h * SB_DH:(h + 1) * SB_DH]
    o_ref[0] = _rms(o, norm_ref[...]).astype(BF16)


def _sb_sample(page_table, bias_rows, q_s, k_new, v_new, pool_kt, pool_vt, norm, layer, n_new):
    n_seq, n_pages = page_table.shape
    qrows = n_new * SB_HEADS
    assert PAGE_SIZE == LANES and n_new <= SUBLANES
    up = jnp.asarray(_sb_scan_matrix(), BF16)
    seq3 = lambda b, pt: (b, 0, 0)
    const = lambda b, pt: (0, 0)
    page_specs = [pl.BlockSpec((None, None, SB_HEADS, SB_DH, PAGE_SIZE),
                               lambda b, pt, g=g: (layer, pt[b * n_pages + g], 0, 0, 0)) for g in range(n_pages)]
    grid_spec = pltpu.PrefetchScalarGridSpec(
        num_scalar_prefetch=1,
        grid=(n_seq,),
        in_specs=[
            pl.BlockSpec(bias_rows.shape, const),
            pl.BlockSpec((1, SUBLANES, SB_W), seq3),
            pl.BlockSpec((1, SUBLANES, SB_W), seq3),
            pl.BlockSpec((1, SUBLANES, SB_W), seq3),
            *page_specs, *page_specs,
            pl.BlockSpec((1, SB_DH), const),
            pl.BlockSpec(up.shape, const),
        ],
        out_specs=pl.BlockSpec((1, qrows, SB_DH), seq3),
    )
    return pl.pallas_call(
        functools.partial(_sb_sample_body, n_new, n_pages),
        grid_spec=grid_spec,
        out_shape=jax.ShapeDtypeStruct((n_seq, qrows, SB_DH), BF16),
        compiler_params=pltpu.CompilerParams(dimension_semantics=("arbitrary",),
                                             vmem_limit_bytes=VMEM_LIMIT),
        name="sb_sample",
    )(page_table.reshape(-1), bias_rows, q_s, k_new, v_new, *([pool_kt] * n_pages), *([pool_vt] * n_pages),
      norm, up)


def kernel(x_prompt, x_sample, cache_sb_k, cache_sb_v, state_gla, page_table, p_prompt, p_sample, f1_pre, f1_in, f1_out, f1_post, mix_pre, w_in, gate_w, gate_b, gla_norm, sb_norm, sb_bias, w_out, mix_post, f2_pre, f2_in, f2_out, f2_post, ple_pre, ple_w_gate, ple_w_up, ple_post):
    batch, seq, _ = x_prompt.shape
    n_seq, n_new, _ = x_sample.shape
    depth = f1_in.shape[0]
    n_p, n_s = batch * seq, n_seq * n_new
    assert n_p % TOK_TILE == 0 and n_s % TOK_TILE == 0 and seq % SB_TILE == 0 and n_p % GLA_S_ROWS == 0

    assert seq % TOK_TILE == 0
    x_p, x_s = x_prompt.reshape(n_p, D_MODEL), x_sample.reshape(n_s, D_MODEL)
    vec = lambda a: a.reshape(1, -1)
    lr0 = ZG_W
    sb0 = ZG_W + GLA_RANK
    pk, pv, ps, sk_l, sv_l, ss = [], [], [], [], [], []
    for l in range(depth):
        wcat = jnp.concatenate(
            [w_in[l][:, :lr0], w_in[l][:, sb0:], w_in[l][:, lr0:sb0],
             jnp.zeros((D_MODEL, LANES - GLA_RANK), F32)], axis=1).astype(BF16)
        gw_pad = jnp.concatenate([gate_w[l], jnp.zeros((LANES - GLA_RANK, GLA_QK_W), F32)], axis=0)
        x1, zg, la, sq_b, sk_b, sv_b, sk_t, sv_t, sk_s, sv_s = _ffn_in(
            x_p, x_s, batch, vec(f1_pre[l]), f1_in[l].astype(BF16), f1_out[l].astype(BF16), vec(f1_post[l]),
            vec(mix_pre[l]), wcat, gw_pad, vec(gate_b[l]))

        a_p, s_gla_p = _gla_prompt(zg, la, vec(gla_norm[l]), batch, seq)
        norm_pair = vec(jnp.tile(sb_norm[l], SB_PAIR))
        bias2 = sb_bias[l] * LOG2E
        o_sb_p = _sb_prompt(bias2, sq_b, sk_b, sv_b, norm_pair, batch, seq)

        a_s, s_gla_s = _gla_sample(zg, la, state_gla[l:l + 1], vec(gla_norm[l]), n_p, n_seq, n_new)
        rows8 = lambda a: jnp.pad(a.astype(F32).reshape(n_seq, n_new, SB_W),
                                  ((0, 0), (0, SUBLANES - n_new), (0, 0)))
        bias_rows = jnp.broadcast_to(jnp.tile(bias2, n_new)[:, None], (n_new * SB_HEADS, LANES))
        to_hdk = lambda c: jnp.transpose(c, (0, 1, 3, 4, 2))
        o_sb_s = _sb_sample(page_table, bias_rows, rows8(sq_b[n_p:]), rows8(sk_s), rows8(sv_s),
                            to_hdk(cache_sb_k), to_hdk(cache_sb_v), vec(sb_norm[l]), l, n_new)

        x_p, x_s = _merge_out(
            x1, a_p, a_s, o_sb_p, o_sb_s.reshape(n_s, SB_W), p_prompt[l].reshape(n_p, PLE_DIM),
            p_sample[l].reshape(n_s, PLE_DIM), w_out[l].astype(BF16), vec(mix_post[l]), vec(f2_pre[l]),
            f2_in[l].astype(BF16), f2_out[l].astype(BF16), vec(f2_post[l]), vec(ple_pre[l]),
            ple_w_gate[l].astype(BF16), ple_w_up[l].astype(BF16), vec(ple_post[l]))

        pk.append(jnp.transpose(sk_t, (0, 3, 1, 2)))
        pv.append(jnp.transpose(sv_t, (0, 3, 1, 2)))
        ps.append(s_gla_p)
        sk_l.append(sk_s.reshape(n_seq, n_new, SB_HEADS, SB_DH))
        sv_l.append(sv_s.reshape(n_seq, n_new, SB_HEADS, SB_DH))
        ss.append(s_gla_s[0])
    return (x_p.reshape(batch, seq, D_MODEL), x_s.reshape(n_seq, n_new, D_MODEL),
            jnp.stack(pk), jnp.stack(pv), jnp.stack(ps), jnp.stack(sk_l), jnp.stack(sv_l), jnp.stack(ss))
```

```python
import functools

import numpy as np
import jax
import jax.numpy as jnp
from jax import lax
from jax.experimental import pallas as pl
from jax.experimental.pallas import tpu as pltpu

F32 = jnp.float32
BF16 = jnp.bfloat16

D_MODEL = 1024
D_FF = 2816
PLE_DIM = 256
GLA_HEADS = 4
GLA_DK = 64
GLA_DV = 128
GLA_RANK = 16
GLA_GATE_TEMP = 16.0
GLA_CHUNK = 64
GLA_SUB = 16
SB_HEADS = 8
SB_DH = 64
PAGE_SIZE = 128
EPS = 1e-6
GLA_QK_W = GLA_HEADS * GLA_DK
GLA_V_W = GLA_HEADS * GLA_DV
SB_W = SB_HEADS * SB_DH
ZG_W = 2 * GLA_QK_W + 2 * GLA_V_W
LANES = 128
SUBLANES = 8
EXP_CLAMP = 80.0
LOG2E = 1.4426950408889634

TOK_TILE = 512
FF_TILE = 256
VMEM_LIMIT = 60 * 1024 * 1024


def _log2(n):
    k = int(n).bit_length() - 1
    assert 1 << k == n, n
    return k


def _div(x, n):
    return lax.shift_right_logical(x, _log2(n))


def _mod(x, n):
    assert n & (n - 1) == 0
    return x & (n - 1)


def _iota(shape, dim):
    return lax.broadcasted_iota(jnp.int32, shape, dim)


def _dot(a, b):
    return jnp.dot(a, b, preferred_element_type=F32)


def _dot_nt(a, b):
    return lax.dot_general(a, b, (((1,), (1,)), ((), ())), preferred_element_type=F32)


def _dot01(t_bf16, x):
    hi = x.astype(BF16)
    lo = (x - hi.astype(F32)).astype(BF16)
    return _dot(t_bf16, hi) + _dot(t_bf16, lo)


def _rms(x, w):
    ms = jnp.mean(x * x, axis=-1, keepdims=True)
    return x * lax.rsqrt(ms + EPS) * w


def _sigmoid(x):
    return 1.0 / (1.0 + jnp.exp(-x))


def _softplus2(x):
    return jnp.maximum(x, jnp.log(1.0 + jnp.exp2(jnp.minimum(x, 126.0))) * LOG2E)


def _swiglu_ffn(h, win_ref, wout_ref, acc_ref):
    for j in range(D_FF // FF_TILE):
        g = _dot(h, win_ref[:, j * FF_TILE:(j + 1) * FF_TILE])
        u = _dot(h, win_ref[:, D_FF + j * FF_TILE:D_FF + (j + 1) * FF_TILE])
        act = (g * _sigmoid(g) * u).astype(BF16)
        part = _dot(act, wout_ref[j * FF_TILE:(j + 1) * FF_TILE, :])
        if j == 0:
            acc_ref[...] = part
        else:
            acc_ref[...] += part


def _resident(shape):
    return pl.BlockSpec(shape, lambda i: (0,) * len(shape), pipeline_mode=pl.Buffered(1))


def _ffn_in_body(n_ptiles, xp_ref, xs_ref, pre_ref, win_ref, wout_ref, post_ref, mixpre_ref, wcat_ref, gw_ref,
                 gb_ref, x1_ref, zg_ref, la_ref, sq_ref, skb_ref, svb_ref, skt_ref, svt_ref, sks_ref, svs_ref,
                 acc_ref):
    is_prompt = pl.program_id(0) < n_ptiles
    x = jnp.where(is_prompt, xp_ref[...], xs_ref[...])
    _swiglu_ffn(_rms(x, pre_ref[...]).astype(BF16), win_ref, wout_ref, acc_ref)
    x1 = x + 0.5 * _rms(acc_ref[...], post_ref[...])
    x1_ref[...] = x1
    h2 = _rms(x1, mixpre_ref[...]).astype(BF16)
    zg_ref[...] = _dot(h2, wcat_ref[:, 0:ZG_W])
    sb = _dot(h2, wcat_ref[:, ZG_W:ZG_W + 3 * SB_W])
    sq_ref[...] = (sb[:, 0:SB_W] * (SB_DH ** -0.5 * LOG2E)).astype(BF16)
    sk = sb[:, SB_W:2 * SB_W]
    sv = sb[:, 2 * SB_W:3 * SB_W]
    skb_ref[...] = sk.astype(BF16)
    svb_ref[...] = sv.astype(BF16)

    @pl.when(is_prompt)
    def _():
        skt_ref[0] = sk.T.reshape(SB_HEADS, SB_DH, TOK_TILE)
        svt_ref[0] = sv.T.reshape(SB_HEADS, SB_DH, TOK_TILE)

    @pl.when(jnp.logical_not(is_prompt))
    def _():
        sks_ref[...] = sk
        svs_ref[...] = sv

    lr = _dot(h2, wcat_ref[:, ZG_W + 3 * SB_W:])
    logit = jnp.dot(lr, gw_ref[...], precision=lax.Precision.HIGHEST,
                    preferred_element_type=F32) + gb_ref[...]
    la_ref[...] = (jnp.minimum(logit, 0.0) - jnp.log1p(jnp.exp(-jnp.abs(logit)))) * (1.0 / GLA_GATE_TEMP)


def _ffn_in(x_p, x_s, batch, pre, w_in_b, w_out_b, post, mixpre, wcat_b, gw_pad, gb):
    n_p, n_s = x_p.shape[0], x_s.shape[0]
    n = n_p + n_s
    seq = n_p // batch
    n_pt, per_seq = n_p // TOK_TILE, seq // TOK_TILE
    row = lambda i: (i, 0)
    p_row = lambda i: (jnp.minimum(i, n_pt - 1), 0)
    s_row = lambda i: (jnp.maximum(i - n_pt, 0), 0)
    p_kv = lambda i: (jnp.minimum(i, n_pt - 1) // per_seq, 0, 0, jnp.minimum(i, n_pt - 1) % per_seq)
    vecd = _resident((1, D_MODEL))
    out_shapes = (
        jax.ShapeDtypeStruct((n, D_MODEL), F32),
        jax.ShapeDtypeStruct((n, ZG_W), F32),
        jax.ShapeDtypeStruct((n, GLA_QK_W), F32),
        jax.ShapeDtypeStruct((n, SB_W), BF16),
        jax.ShapeDtypeStruct((n, SB_W), BF16),
        jax.ShapeDtypeStruct((n, SB_W), BF16),
        jax.ShapeDtypeStruct((batch, SB_HEADS, SB_DH, seq), F32),
        jax.ShapeDtypeStruct((batch, SB_HEADS, SB_DH, seq), F32),
        jax.ShapeDtypeStruct((n_s, SB_W), F32),
        jax.ShapeDtypeStruct((n_s, SB_W), F32),
    )
    return pl.pallas_call(
        functools.partial(_ffn_in_body, n_pt),
        grid=(n // TOK_TILE,),
        in_specs=[
            pl.BlockSpec((TOK_TILE, D_MODEL), p_row),
            pl.BlockSpec((TOK_TILE, D_MODEL), s_row),
            vecd,
            _resident(w_in_b.shape),
            _resident(w_out_b.shape),
            vecd, vecd,
            _resident(wcat_b.shape),
            _resident(gw_pad.shape),
            _resident((1, GLA_QK_W)),
        ],
        out_specs=[
            pl.BlockSpec((TOK_TILE, D_MODEL), row),
            pl.BlockSpec((TOK_TILE, ZG_W), row),
            pl.BlockSpec((TOK_TILE, GLA_QK_W), row),
            pl.BlockSpec((TOK_TILE, SB_W), row),
            pl.BlockSpec((TOK_TILE, SB_W), row),
            pl.BlockSpec((TOK_TILE, SB_W), row),
            pl.BlockSpec((1, SB_HEADS, SB_DH, TOK_TILE), p_kv),
            pl.BlockSpec((1, SB_HEADS, SB_DH, TOK_TILE), p_kv),
            pl.BlockSpec((TOK_TILE, SB_W), s_row),
            pl.BlockSpec((TOK_TILE, SB_W), s_row),
        ],
        out_shape=out_shapes,
        scratch_shapes=[pltpu.VMEM((TOK_TILE, D_MODEL), F32)],
        compiler_params=pltpu.CompilerParams(dimension_semantics=("arbitrary",),
                                             vmem_limit_bytes=VMEM_LIMIT),
        name="ffn_in",
    )(x_p, x_s, pre, w_in_b, w_out_b, post, mixpre, wcat_b, gw_pad, gb)


def _merge_out_body(n_ptiles, x1_ref, ap_ref, as_ref, sp_ref, ss_ref, pp_ref, ps_ref, wout_ref, mixpost_ref,
                    pre_ref, win_ref, woutf_ref, post_ref, plepre_ref, pleg_ref, pleu_ref, plepost_ref,
                    yp_ref, ys_ref, acc_ref):
    is_prompt = pl.program_id(0) < n_ptiles
    a = jnp.where(is_prompt, ap_ref[...], as_ref[...])
    s = jnp.where(is_prompt, sp_ref[...], ss_ref[...])
    p = jnp.where(is_prompt, pp_ref[...], ps_ref[...])
    m = _dot(a, wout_ref[0:GLA_V_W, :]) + _dot(s, wout_ref[GLA_V_W:, :])
    x2 = x1_ref[...] + _rms(m, mixpost_ref[...])
    _swiglu_ffn(_rms(x2, pre_ref[...]).astype(BF16), win_ref, woutf_ref, acc_ref)
    x3 = x2 + 0.5 * _rms(acc_ref[...], post_ref[...])
    hg = _rms(x3, plepre_ref[...]).astype(BF16)
    gate = _sigmoid(_dot(hg, pleg_ref[...]))
    e = _dot(p.astype(BF16), pleu_ref[...])
    y = x3 + _rms(gate * e, plepost_ref[...])

    @pl.when(is_prompt)
    def _():
        yp_ref[...] = y

    @pl.when(jnp.logical_not(is_prompt))
    def _():
        ys_ref[...] = y


def _merge_out(x1, a_p, a_s, s_p, s_s, p_p, p_s, wout_b, mixpost, pre, w_in_b, w_out_b, post, plepre, pleg_b,
               pleu_b, plepost):
    n_p, n_s = a_p.shape[0], a_s.shape[0]
    n_pt = n_p // TOK_TILE
    row = lambda i: (i, 0)
    p_row = lambda i: (jnp.minimum(i, n_pt - 1), 0)
    s_row = lambda i: (jnp.maximum(i - n_pt, 0), 0)
    pair = lambda width: [pl.BlockSpec((TOK_TILE, width), p_row), pl.BlockSpec((TOK_TILE, width), s_row)]
    vecd = _resident((1, D_MODEL))
    return pl.pallas_call(
        functools.partial(_merge_out_body, n_pt),
        grid=((n_p + n_s) // TOK_TILE,),
        in_specs=[
            pl.BlockSpec((TOK_TILE, D_MODEL), row),
            *pair(GLA_V_W), *pair(SB_W), *pair(PLE_DIM),
            _resident(wout_b.shape),
            vecd, vecd,
            _resident(w_in_b.shape),
            _resident(w_out_b.shape),
            vecd, vecd,
            _resident(pleg_b.shape),
            _resident(pleu_b.shape),
            vecd,
        ],
        out_specs=pair(D_MODEL),
        out_shape=(jax.ShapeDtypeStruct((n_p, D_MODEL), F32), jax.ShapeDtypeStruct((n_s, D_MODEL), F32)),
        scratch_shapes=[pltpu.VMEM((TOK_TILE, D_MODEL), F32)],
        compiler_params=pltpu.CompilerParams(dimension_semantics=("arbitrary",),
                                             vmem_limit_bytes=VMEM_LIMIT),
        name="merge_out",
    )(x1, a_p, a_s, s_p, s_s, p_p, p_s, wout_b, mixpost, pre, w_in_b, w_out_b, post, plepre, pleg_b, pleu_b,
      plepost)


GLA_BLOCK = 256
GLA_PAIR = 128


def _cum_matrices(block, chunk, sub=None):
    t = np.arange(block)[:, None]
    s = np.arange(block)[None, :]
    same = (t // chunk) == (s // chunk)
    mats = [same & (s <= t)]
    if sub is not None:
        mats.append(same & ((s % chunk) < sub * ((t % chunk) // sub)))
    mats.append(same)
    return np.concatenate(mats, axis=0).astype(np.float32)


def _gla_prompt_body(zg_ref, la_ref, norm_ref, cum_ref, a_ref, sout_ref, st_ref):
    i = pl.program_id(1)

    @pl.when(i == 0)
    def _():
        st_ref[...] = jnp.zeros_like(st_ref)

    nb = GLA_BLOCK
    cums = _dot01(cum_ref[...], la_ref[...])
    b = cums[0:nb]
    r = cums[nb:2 * nb]
    bl = cums[2 * nb:3 * nb]
    q = zg_ref[:, 0:GLA_QK_W] * GLA_DK ** -0.5
    k = zg_ref[:, GLA_QK_W:2 * GLA_QK_W]
    q_intra = q * jnp.exp(b - r)
    q_inter = q * jnp.exp(b)
    k_state = (k * jnp.exp(bl - b)).astype(BF16)
    decay = jnp.exp(bl)

    C, S = GLA_CHUNK, GLA_SUB
    n_sub = C // S
    hc = GLA_HEADS * C
    head_sel = _div(_iota((hc, GLA_QK_W), 1), GLA_DK) == _div(_iota((hc, GLA_QK_W), 0), C)
    tt = _mod(_iota((hc, n_sub * C), 0), C)
    cc = _iota((hc, n_sub * C), 1)
    att_keep = (_div(cc, C) == _div(tt, S)) & (_mod(cc, C) <= tt)
    srow = _iota((C, GLA_QK_W), 0)
    pair_chunk = _div(_iota((GLA_DV, GLA_PAIR), 1), C)
    st_lane_head = _div(_iota((GLA_DV, GLA_QK_W), 1), GLA_DK)

    for c in range(nb // C):
        rows = slice(c * C, (c + 1) * C)
        pair = slice((c // 2) * GLA_PAIR, (c // 2 + 1) * GLA_PAIR)
        b_c, k_c = b[rows], k[rows]
        kst = []
        for sidx in range(n_sub):
            r_row = r[c * C + sidx * S:c * C + sidx * S + 1, :]
            kk = k_c * jnp.exp(jnp.minimum(r_row - b_c, EXP_CLAMP))
            kst.append(jnp.where(srow < (sidx + 1) * S, kk, 0.0))
        kst = jnp.concatenate(kst, axis=0).astype(BF16)
        qst = jnp.where(head_sel, jnp.concatenate([q_intra[rows]] * GLA_HEADS, axis=0), 0.0).astype(BF16)
        att = jnp.where(att_keep, _dot_nt(qst, kst), 0.0).astype(BF16)
        qin = jnp.where(head_sel, jnp.concatenate([q_inter[rows]] * GLA_HEADS, axis=0), 0.0).astype(BF16)
        o_inter = _dot_nt(qin, st_ref[...].astype(BF16))
        upd = jnp.zeros((GLA_DV, GLA_QK_W), F32)
        for h in range(GLA_HEADS):
            vcol = slice(2 * GLA_QK_W + h * GLA_DV, 2 * GLA_QK_W + (h + 1) * GLA_DV)
            gcol = slice(2 * GLA_QK_W + GLA_V_W + h * GLA_DV, 2 * GLA_QK_W + GLA_V_W + (h + 1) * GLA_DV)
            v_h = zg_ref[rows, vcol].astype(BF16)
            o = _dot(att[h * C:(h + 1) * C], jnp.concatenate([v_h] * n_sub, axis=0)) + o_inter[h * C:(h + 1) * C]
            gate = zg_ref[rows, gcol]
            a_ref[rows, h * GLA_DV:(h + 1) * GLA_DV] = (
                _rms(o, norm_ref[...]) * (gate * _sigmoid(gate))).astype(BF16)
            v_t = zg_ref[pair, vcol].T
            u_t = _dot(jnp.where(pair_chunk == (c % 2), v_t, 0.0).astype(BF16), k_state[pair])
            upd = upd + jnp.where(st_lane_head == h, u_t, 0.0)
        st_ref[...] = st_ref[...] * decay[c * C:c * C + 1, :] + upd

    @pl.when(i == pl.num_programs(1) - 1)
    def _():
        sout_ref[...] = st_ref[...].T.reshape(1, GLA_HEADS, GLA_DK, GLA_DV)


def _gla_prompt(zg, la, norm, batch, seq):
    nblk = seq // GLA_BLOCK
    cum = jnp.asarray(_cum_matrices(GLA_BLOCK, GLA_CHUNK, GLA_SUB), BF16)
    row = lambda b, i: (b * nblk + i, 0)
    return pl.pallas_call(
        _gla_prompt_body,
        grid=(batch, nblk),
        in_specs=[
            pl.BlockSpec((GLA_BLOCK, ZG_W), row),
            pl.BlockSpec((GLA_BLOCK, GLA_QK_W), row),
            pl.BlockSpec((1, GLA_DV), lambda b, i: (0, 0)),
            pl.BlockSpec(cum.shape, lambda b, i: (0, 0)),
        ],
        out_specs=[
            pl.BlockSpec((GLA_BLOCK, GLA_V_W), row),
            pl.BlockSpec((1, GLA_HEADS, GLA_DK, GLA_DV), lambda b, i: (b, 0, 0, 0)),
        ],
        out_shape=(jax.ShapeDtypeStruct((batch * seq, GLA_V_W), BF16),
                   jax.ShapeDtypeStruct((batch, GLA_HEADS, GLA_DK, GLA_DV), F32)),
        scratch_shapes=[pltpu.VMEM((GLA_DV, GLA_QK_W), F32)],
        compiler_params=pltpu.CompilerParams(dimension_semantics=("arbitrary", "arbitrary"),
                                             vmem_limit_bytes=VMEM_LIMIT),
        name="gla_prompt",
    )(zg, la, norm, cum)


GLA_S_ROWS = 128


def _gla_sample_body(n_new, zg_ref, la_ref, s_ref, norm_ref, cum_ref, a_ref, sout_ref,
                     qin_ref, khat_ref, dec_ref, oint_ref, vt_ref):
    rows = GLA_S_ROWS
    cums = _dot01(cum_ref[...], la_ref[...])
    b = cums[0:rows]
    bl = cums[rows:2 * rows]
    q = zg_ref[:, 0:GLA_QK_W] * GLA_DK ** -0.5
    k = zg_ref[:, GLA_QK_W:2 * GLA_QK_W]
    q_dec = q * jnp.exp(b)
    k_intra = (k * jnp.exp(jnp.minimum(-b, EXP_CLAMP))).astype(BF16)
    qin_ref[...] = q_dec
    khat_ref[...] = (k * jnp.exp(bl - b)).astype(BF16)
    dec_ref[...] = jnp.exp(bl)

    hr = GLA_HEADS * rows
    head_sel = _div(_iota((hr, GLA_QK_W), 1), GLA_DK) == _div(_iota((hr, GLA_QK_W), 0), rows)
    qst = jnp.where(head_sel, jnp.concatenate([q_dec] * GLA_HEADS, axis=0), 0.0).astype(BF16)
    tt = _mod(_iota((hr, rows), 0), rows)
    ss = _iota((hr, rows), 1)
    keep = (_div(tt, n_new) == _div(ss, n_new)) & (ss <= tt)
    att = jnp.where(keep, _dot_nt(qst, k_intra), 0.0).astype(BF16)
    for h in range(GLA_HEADS):
        vcol = slice(2 * GLA_QK_W + h * GLA_DV, 2 * GLA_QK_W + (h + 1) * GLA_DV)
        v_h = zg_ref[:, vcol]
        oint_ref[:, h * GLA_DV:(h + 1) * GLA_DV] = _dot(att[h * rows:(h + 1) * rows], v_h.astype(BF16))
        vt_ref[h] = v_h.T

    per8 = SUBLANES // n_new
    sub8 = _div(_iota((SUBLANES, GLA_QK_W), 0), n_new)
    lane8 = _div(_iota((SUBLANES, GLA_QK_W), 1), GLA_DK)
    out8 = _div(_iota((SUBLANES, GLA_DV), 0), n_new)
    vt_seq = _div(_iota((GLA_DV, rows), 1), n_new)
    st_lane_head = _div(_iota((GLA_DV, GLA_QK_W), 1), GLA_DK)

    def group_step(m, carry):
        r8 = pl.ds(pl.multiple_of(m * SUBLANES, SUBLANES), SUBLANES)
        q8 = qin_ref[r8, :]
        d8 = dec_ref[r8, :]
        res = []
        for e in range(per8):
            n = m * per8 + e
            s_n = s_ref[0, n].reshape(GLA_HEADS * GLA_DK, GLA_DV)
            lhs = jnp.concatenate([jnp.where((sub8 == e) & (lane8 == h), q8, 0.0) for h in range(GLA_HEADS)],
                                  axis=0)
            res.append(_dot(lhs.astype(BF16), s_n.astype(BF16)))
            upd = jnp.zeros((GLA_DV, GLA_QK_W), F32)
            for h in range(GLA_HEADS):
                u_t = _dot(jnp.where(vt_seq == n, vt_ref[h], 0.0).astype(BF16), khat_ref[...])
                upd = upd + jnp.where(st_lane_head == h, u_t, 0.0)
            s_new_t = s_n.T * d8[e * n_new:e * n_new + 1, :] + upd
            sout_ref[0, n] = s_new_t.T.reshape(GLA_HEADS, GLA_DK, GLA_DV)
        for h in range(GLA_HEADS):
            o8 = res[0][h * SUBLANES:(h + 1) * SUBLANES]
            for e in range(1, per8):
                o8 = jnp.where(out8 == e, res[e][h * SUBLANES:(h + 1) * SUBLANES], o8)
            oint_ref[r8, h * GLA_DV:(h + 1) * GLA_DV] += o8
        return carry

    lax.fori_loop(0, rows // SUBLANES, group_step, 0)

    for h in range(GLA_HEADS):
        gcol = slice(2 * GLA_QK_W + GLA_V_W + h * GLA_DV, 2 * GLA_QK_W + GLA_V_W + (h + 1) * GLA_DV)
        gate = zg_ref[:, gcol]
        o = oint_ref[:, h * GLA_DV:(h + 1) * GLA_DV]
        a_ref[:, h * GLA_DV:(h + 1) * GLA_DV] = (_rms(o, norm_ref[...]) * (gate * _sigmoid(gate))).astype(BF16)


def _gla_sample(zg, la, state, norm, row0, n_seq, n_new):
    assert SUBLANES % n_new == 0 and GLA_S_ROWS % n_new == 0
    seqs = GLA_S_ROWS // n_new
    steps = n_seq // seqs
    blk0 = row0 // GLA_S_ROWS
    cum = jnp.asarray(_cum_matrices(GLA_S_ROWS, n_new), BF16)
    return pl.pallas_call(
        functools.partial(_gla_sample_body, n_new),
        grid=(steps,),
        in_specs=[
            pl.BlockSpec((GLA_S_ROWS, ZG_W), lambda i: (blk0 + i, 0)),
            pl.BlockSpec((GLA_S_ROWS, GLA_QK_W), lambda i: (blk0 + i, 0)),
            pl.BlockSpec((1, seqs, GLA_HEADS, GLA_DK, GLA_DV), lambda i: (0, i, 0, 0, 0)),
            pl.BlockSpec((1, GLA_DV), lambda i: (0, 0)),
            pl.BlockSpec(cum.shape, lambda i: (0, 0)),
        ],
        out_specs=[
            pl.BlockSpec((GLA_S_ROWS, GLA_V_W), lambda i: (i, 0)),
            pl.BlockSpec((1, seqs, GLA_HEADS, GLA_DK, GLA_DV), lambda i: (0, i, 0, 0, 0)),
        ],
        out_shape=(jax.ShapeDtypeStruct((n_seq * n_new, GLA_V_W), BF16),
                   jax.ShapeDtypeStruct((1, n_seq, GLA_HEADS, GLA_DK, GLA_DV), F32)),
        scratch_shapes=[
            pltpu.VMEM((GLA_S_ROWS, GLA_QK_W), F32),
            pltpu.VMEM((GLA_S_ROWS, GLA_QK_W), BF16),
            pltpu.VMEM((GLA_S_ROWS, GLA_QK_W), F32),
            pltpu.VMEM((GLA_S_ROWS, GLA_V_W), F32),
            pltpu.VMEM((GLA_HEADS, GLA_DV, GLA_S_ROWS), F32),
        ],
        compiler_params=pltpu.CompilerParams(dimension_semantics=("arbitrary",),
                                             vmem_limit_bytes=VMEM_LIMIT),
        name="gla_sample",
    )(zg, la, state, norm, cum)


SB_TILE = 256
SB_PAIR = LANES // SB_DH
SB_HPS = 4
SB_BLK = SB_HPS * SB_DH
SB_GROUP = 2
SB_MASKED = -1e30


def _sb_prompt_body(bias_ref, q_ref, k_ref, v_ref, norm_ref, u_ref, o_ref, acc_ref, carry_ref, bias_buf, zs_buf,
                    cs_buf, tot_buf):
    p = pl.program_id(1)
    qi = pl.program_id(2)
    T = SB_TILE
    lane_head = _div(_iota((T, SB_BLK), 1), SB_DH)
    q = q_ref[...]
    qh = [jnp.where(lane_head == e, q, jnp.zeros_like(q)) for e in range(SB_HPS)]
    u = u_ref[...]

    @pl.when(qi == 0)
    def _():
        valid = _iota((T, T), 1) < _iota((T, T), 0)
        for e in range(SB_HPS):
            b = bias_ref[p * SB_HPS + e]
            bias_buf[e, 0] = jnp.where(valid, b, SB_MASKED)
            bias_buf[e, 1] = jnp.full((T, T), b, F32)
            bias_buf[e, 2] = jnp.full((T, T), SB_MASKED, F32)

    acc_ref[...] = jnp.zeros_like(acc_ref)
    carry_ref[...] = jnp.zeros_like(carry_ref)

    def tile_of(t):
        kj = qi - t
        kind = jnp.where(t == 0, 0, jnp.where(kj >= 0, 1, 2))
        return kind, pl.ds(pl.multiple_of(jnp.maximum(kj, 0) * T, T), T)

    def logits_stage(g):
        for j in range(SB_GROUP):
            kind, rows = tile_of(g * SB_GROUP + j)
            kt = k_ref[rows, :]
            for e in range(SB_HPS):
                z = _dot_nt(qh[e], kt) + bias_buf[e, kind]
                sp = _softplus2(z)
                zs_buf[j, e] = z - sp
                cs = _dot(sp.astype(BF16), u)
                cs_buf[j, e] = cs
                tot_buf[j, e] = cs[:, 0:1] + sp[:, 0:1]

    def values_stage(g):
        for j in range(SB_GROUP):
            _, rows = tile_of(g * SB_GROUP + j)
            vt = v_ref[rows, :]
            for e in range(SB_HPS):
                w = jnp.exp2(zs_buf[j, e] - cs_buf[j, e])
                own = slice((e // SB_PAIR) * LANES, (e // SB_PAIR + 1) * LANES)
                acc_ref[e] += jnp.exp2(-carry_ref[e]) * _dot(w.astype(BF16), vt)[:, own]
                carry_ref[e] += tot_buf[j, e]

    n_groups = _div(qi + SB_GROUP, SB_GROUP)
    logits_stage(0)

    def step(i, c):
        values_stage(i - 1)
        logits_stage(i)
        return c

    lax.fori_loop(1, n_groups, step, 0)
    values_stage(n_groups - 1)

    half_head = _div(_iota((T, LANES), 1), SB_DH)
    for blk in range(SB_HPS // SB_PAIR):
        o = acc_ref[blk * SB_PAIR]
        for e in range(1, SB_PAIR):
            o = jnp.where(half_head == e, acc_ref[blk * SB_PAIR + e], o)
        sq = o * o
        ms = jnp.zeros_like(o)
        for e in range(SB_PAIR):
            ms_e = jnp.sum(jnp.where(half_head == e, sq, 0.0), axis=1, keepdims=True) * (1.0 / SB_DH)
            ms = jnp.where(half_head == e, ms_e, ms)
        lanes = slice(blk * LANES, (blk + 1) * LANES)
        o_ref[:, lanes] = (o * lax.rsqrt(ms + EPS) * norm_ref[:, lanes]).astype(BF16)


def _sb_prompt(bias, sq_b, sk_b, sv_b, norm_blk, batch, seq):
    nq = seq // SB_TILE
    nblk = SB_HEADS // SB_HPS
    j = np.arange(SB_TILE)[:, None]
    s = np.arange(SB_TILE)[None, :]
    u = jnp.asarray((j > s).astype(np.float32), BF16)
    return pl.pallas_call(
        _sb_prompt_body,
        grid=(batch, nblk, nq),
        in_specs=[
            pl.BlockSpec(memory_space=pltpu.SMEM),
            pl.BlockSpec((SB_TILE, SB_BLK), lambda b, p, i: (b * nq + i, p)),
            pl.BlockSpec((seq, SB_BLK), lambda b, p, i: (b, p)),
            pl.BlockSpec((seq, SB_BLK), lambda b, p, i: (b, p)),
            pl.BlockSpec((1, SB_BLK), lambda b, p, i: (0, 0)),
            pl.BlockSpec((SB_TILE, SB_TILE), lambda b, p, i: (0, 0)),
        ],
        out_specs=pl.BlockSpec((SB_TILE, SB_BLK), lambda b, p, i: (b * nq + i, p)),
        out_shape=jax.ShapeDtypeStruct((batch * seq, SB_W), BF16),
        scratch_shapes=[
            pltpu.VMEM((SB_HPS, SB_TILE, LANES), F32),
            pltpu.VMEM((SB_HPS, SB_TILE, 1), F32),
            pltpu.VMEM((SB_HPS, 3, SB_TILE, SB_TILE), F32),
            pltpu.VMEM((SB_GROUP, SB_HPS, SB_TILE, SB_TILE), F32),
            pltpu.VMEM((SB_GROUP, SB_HPS, SB_TILE, SB_TILE), F32),
            pltpu.VMEM((SB_GROUP, SB_HPS, SB_TILE, 1), F32),
        ],
        compiler_params=pltpu.CompilerParams(dimension_semantics=("arbitrary", "arbitrary", "arbitrary"),
                                             vmem_limit_bytes=VMEM_LIMIT),
        name="sb_prompt",
    )(bias, sq_b, sk_b, sv_b, norm_blk, u)


def _sb_scan_matrix():
    s_in = np.arange(LANES)[:, None]
    s_out = np.arange(LANES)[None, :]
    return np.concatenate([s_in > s_out, np.ones((LANES, LANES), bool)], axis=1).astype(np.float32)


def _sb_sample_body(n_new, n_pages, pt_ref, bias_ref, q_ref, kn_ref, vn_ref, *rest):
    del pt_ref
    k_refs, v_refs = rest[:n_pages], rest[n_pages:2 * n_pages]
    norm_ref, up_ref, o_ref = rest[2 * n_pages:]
    qrows = n_new * SB_HEADS
    own_head = _mod(_iota((qrows, SB_W), 0), SB_HEADS) == _div(_iota((qrows, SB_W), 1), SB_DH)
    q_rows = jnp.concatenate([jnp.broadcast_to(q_ref[0, t:t + 1, :], (SB_HEADS, SB_W)) for t in range(n_new)],
                             axis=0)
    qbd = jnp.where(own_head, q_rows, 0.0).astype(BF16)
    bias = bias_ref[...]

    zs = [_dot(qbd, k_refs[g][...].reshape(SB_W, PAGE_SIZE).astype(BF16)) + bias for g in range(n_pages)]
    pad = jnp.zeros((LANES - SUBLANES, SB_W), F32)
    kn = jnp.concatenate([kn_ref[0], pad], axis=0).astype(BF16)
    vn = jnp.concatenate([vn_ref[0], pad], axis=0).astype(BF16)
    zs.append(_dot_nt(qbd, kn) + bias)
    new_ok = _iota((qrows, LANES), 1) < _div(_iota((qrows, LANES), 0), SB_HEADS)
    sps = [_softplus2(z) for z in zs]
    sps[n_pages] = jnp.where(new_ok, sps[n_pages], 0.0)
    r = _dot(jnp.concatenate(sps, axis=0).astype(BF16), up_ref[...])
    run = jnp.zeros((qrows, LANES), F32)
    acc = jnp.zeros((qrows, SB_W), F32)
    for g in reversed(range(n_pages + 1)):
        blk = r[g * qrows:(g + 1) * qrows]
        w = jnp.exp2(zs[g] - sps[g] - blk[:, 0:LANES] - run)
        run = run + blk[:, LANES:2 * LANES]
        if g == n_pages:
            acc = acc + _dot(jnp.where(new_ok, w, 0.0).astype(BF16), vn)
        else:
            acc = acc + _dot_nt(w.astype(BF16), v_refs[g][...].reshape(SB_W, PAGE_SIZE).astype(BF16))
    acc = jnp.where(own_head, acc, 0.0)
    o = acc[:, 0:SB_DH]
    for h in range(1, SB_HEADS):
        o = o + acc[:, h * SB_DH:(h + 1) * SB_DH]
    o_ref[0] = _rms(o, norm_ref[...]).astype(BF16)


def _sb_sample(page_table, bias_rows, q_s, k_new, v_new, pool_kt, pool_vt, norm, layer, n_new):
    n_seq, n_pages = page_table.shape
    qrows = n_new * SB_HEADS
    assert PAGE_SIZE == LANES and n_new <= SUBLANES
    up = jnp.asarray(_sb_scan_matrix(), BF16)
    seq3 = lambda b, pt: (b, 0, 0)
    const = lambda b, pt: (0, 0)
    page_specs = [pl.BlockSpec((None, None, SB_HEADS, SB_DH, PAGE_SIZE),
                               lambda b, pt, g=g: (layer, pt[b * n_pages + g], 0, 0, 0)) for g in range(n_pages)]
    grid_spec = pltpu.PrefetchScalarGridSpec(
        num_scalar_prefetch=1,
        grid=(n_seq,),
        in_specs=[
            pl.BlockSpec(bias_rows.shape, const),
            pl.BlockSpec((1, SUBLANES, SB_W), seq3),
            pl.BlockSpec((1, SUBLANES, SB_W), seq3),
            pl.BlockSpec((1, SUBLANES, SB_W), seq3),
            *page_specs, *page_specs,
            pl.BlockSpec((1, SB_DH), const),
            pl.BlockSpec(up.shape, const),
        ],
        out_specs=pl.BlockSpec((1, qrows, SB_DH), seq3),
    )
    return pl.pallas_call(
        functools.partial(_sb_sample_body, n_new, n_pages),
        grid_spec=grid_spec,
        out_shape=jax.ShapeDtypeStruct((n_seq, qrows, SB_DH), BF16),
        compiler_params=pltpu.CompilerParams(dimension_semantics=("arbitrary",),
                                             vmem_limit_bytes=VMEM_LIMIT),
        name="sb_sample",
    )(page_table.reshape(-1), bias_rows, q_s, k_new, v_new, *([pool_kt] * n_pages), *([pool_vt] * n_pages),
      norm, up)


def kernel(x_prompt, x_sample, cache_sb_k, cache_sb_v, state_gla, page_table, p_prompt, p_sample, f1_pre, f1_in, f1_out, f1_post, mix_pre, w_in, gate_w, gate_b, gla_norm, sb_norm, sb_bias, w_out, mix_post, f2_pre, f2_in, f2_out, f2_post, ple_pre, ple_w_gate, ple_w_up, ple_post):
    batch, seq, _ = x_prompt.shape
    n_seq, n_new, _ = x_sample.shape
    depth = f1_in.shape[0]
    n_p, n_s = batch * seq, n_seq * n_new
    assert n_p % TOK_TILE == 0 and n_s % TOK_TILE == 0 and seq % SB_TILE == 0 and n_p % GLA_S_ROWS == 0

    assert seq % TOK_TILE == 0
    x_p, x_s = x_prompt.reshape(n_p, D_MODEL), x_sample.reshape(n_s, D_MODEL)
    vec = lambda a: a.reshape(1, -1)
    lr0 = ZG_W
    sb0 = ZG_W + GLA_RANK
    pk, pv, ps, sk_l, sv_l, ss = [], [], [], [], [], []
    for l in range(depth):
        wcat = jnp.concatenate(
            [w_in[l][:, :lr0], w_in[l][:, sb0:], w_in[l][:, lr0:sb0],
             jnp.zeros((D_MODEL, LANES - GLA_RANK), F32)], axis=1).astype(BF16)
        gw_pad = jnp.concatenate([gate_w[l], jnp.zeros((LANES - GLA_RANK, GLA_QK_W), F32)], axis=0)
        x1, zg, la, sq_b, sk_b, sv_b, sk_t, sv_t, sk_s, sv_s = _ffn_in(
            x_p, x_s, batch, vec(f1_pre[l]), f1_in[l].astype(BF16), f1_out[l].astype(BF16), vec(f1_post[l]),
            vec(mix_pre[l]), wcat, gw_pad, vec(gate_b[l]))

        a_p, s_gla_p = _gla_prompt(zg, la, vec(gla_norm[l]), batch, seq)
        bias2 = sb_bias[l] * LOG2E
        o_sb_p = _sb_prompt(bias2, sq_b, sk_b, sv_b, vec(jnp.tile(sb_norm[l], SB_HPS)), batch, seq)

        a_s, s_gla_s = _gla_sample(zg, la, state_gla[l:l + 1], vec(gla_norm[l]), n_p, n_seq, n_new)
        rows8 = lambda a: jnp.pad(a.astype(F32).reshape(n_seq, n_new, SB_W),
                                  ((0, 0), (0, SUBLANES - n_new), (0, 0)))
        bias_rows = jnp.broadcast_to(jnp.tile(bias2, n_new)[:, None], (n_new * SB_HEADS, LANES))
        to_hdk = lambda c: jnp.transpose(c, (0, 1, 3, 4, 2))
        o_sb_s = _sb_sample(page_table, bias_rows, rows8(sq_b[n_p:]), rows8(sk_s), rows8(sv_s),
                            to_hdk(cache_sb_k), to_hdk(cache_sb_v), vec(sb_norm[l]), l, n_new)

        x_p, x_s = _merge_out(
            x1, a_p, a_s, o_sb_p, o_sb_s.reshape(n_s, SB_W), p_prompt[l].reshape(n_p, PLE_DIM),
            p_sample[l].reshape(n_s, PLE_DIM), w_out[l].astype(BF16), vec(mix_post[l]), vec(f2_pre[l]),
            f2_in[l].astype(BF16), f2_out[l].astype(BF16), vec(f2_post[l]), vec(ple_pre[l]),
            ple_w_gate[l].astype(BF16), ple_w_up[l].astype(BF16), vec(ple_post[l]))

        pk.append(jnp.transpose(sk_t, (0, 3, 1, 2)))
        pv.append(jnp.transpose(sv_t, (0, 3, 1, 2)))
        ps.append(s_gla_p)
        sk_l.append(sk_s.reshape(n_seq, n_new, SB_HEADS, SB_DH))
        sv_l.append(sv_s.reshape(n_seq, n_new, SB_HEADS, SB_DH))
        ss.append(s_gla_s[0])
    return (x_p.reshape(batch, seq, D_MODEL), x_s.reshape(n_seq, n_new, D_MODEL),
            jnp.stack(pk), jnp.stack(pv), jnp.stack(ps), jnp.stack(sk_l), jnp.stack(sv_l), jnp.stack(ss))
```

```python
import functools

import numpy as np
import jax
import jax.numpy as jnp
from jax import lax
from jax.experimental import pallas as pl
from jax.experimental.pallas import tpu as pltpu

F32 = jnp.float32
BF16 = jnp.bfloat16

D_MODEL = 1024
D_FF = 2816
PLE_DIM = 256
GLA_HEADS = 4
GLA_DK = 64
GLA_DV = 128
GLA_RANK = 16
GLA_GATE_TEMP = 16.0
GLA_CHUNK = 64
GLA_SUB = 16
SB_HEADS = 8
SB_DH = 64
PAGE_SIZE = 128
EPS = 1e-6
GLA_QK_W = GLA_HEADS * GLA_DK
GLA_V_W = GLA_HEADS * GLA_DV
SB_W = SB_HEADS * SB_DH
ZG_W = 2 * GLA_QK_W + 2 * GLA_V_W
LANES = 128
SUBLANES = 8
EXP_CLAMP = 80.0
LOG2E = 1.4426950408889634

TOK_TILE = 512
FF_TILE = 256
VMEM_LIMIT = 62 * 1024 * 1024


def _log2(n):
    k = int(n).bit_length() - 1
    assert 1 << k == n, n
    return k


def _div(x, n):
    return lax.shift_right_logical(x, _log2(n))


def _mod(x, n):
    assert n & (n - 1) == 0
    return x & (n - 1)


def _iota(shape, dim):
    return lax.broadcasted_iota(jnp.int32, shape, dim)


def _dot(a, b):
    return jnp.dot(a, b, preferred_element_type=F32)


def _dot_nt(a, b):
    return lax.dot_general(a, b, (((1,), (1,)), ((), ())), preferred_element_type=F32)


def _dot01(t_bf16, x):
    hi = x.astype(BF16)
    lo = (x - hi.astype(F32)).astype(BF16)
    return _dot(t_bf16, hi) + _dot(t_bf16, lo)


def _rms(x, w):
    ms = jnp.mean(x * x, axis=-1, keepdims=True)
    return x * lax.rsqrt(ms + EPS) * w


def _sigmoid(x):
    return 1.0 / (1.0 + jnp.exp(-x))


def _softplus2(x):
    return jnp.maximum(x, jnp.log(1.0 + jnp.exp2(jnp.minimum(x, 126.0))) * LOG2E)


def _swiglu_ffn(h, win_ref, wout_ref, acc_ref):
    for j in range(D_FF // FF_TILE):
        g = _dot(h, win_ref[:, j * FF_TILE:(j + 1) * FF_TILE])
        u = _dot(h, win_ref[:, D_FF + j * FF_TILE:D_FF + (j + 1) * FF_TILE])
        act = (g * _sigmoid(g) * u).astype(BF16)
        part = _dot(act, wout_ref[j * FF_TILE:(j + 1) * FF_TILE, :])
        if j == 0:
            acc_ref[...] = part
        else:
            acc_ref[...] += part


def _resident(shape):
    return pl.BlockSpec(shape, lambda i: (0,) * len(shape), pipeline_mode=pl.Buffered(1))


def _ffn_in_body(n_ptiles, xp_ref, xs_ref, pre_ref, win_ref, wout_ref, post_ref, mixpre_ref, wcat_ref, gw_ref,
                 gb_ref, x1_ref, zg_ref, la_ref, sq_ref, skb_ref, svb_ref, skt_ref, svt_ref, sks_ref, svs_ref,
                 acc_ref):
    is_prompt = pl.program_id(0) < n_ptiles
    x = jnp.where(is_prompt, xp_ref[...], xs_ref[...])
    _swiglu_ffn(_rms(x, pre_ref[...]).astype(BF16), win_ref, wout_ref, acc_ref)
    x1 = x + 0.5 * _rms(acc_ref[...], post_ref[...])
    x1_ref[...] = x1
    h2 = _rms(x1, mixpre_ref[...]).astype(BF16)
    sb = _dot(h2, wcat_ref[:, ZG_W:ZG_W + 3 * SB_W])
    sq_ref[...] = (sb[:, 0:SB_W] * (SB_DH ** -0.5 * LOG2E)).astype(BF16)
    sk = sb[:, SB_W:2 * SB_W]
    sv = sb[:, 2 * SB_W:3 * SB_W]
    skb_ref[...] = sk.astype(BF16)
    svb_ref[...] = sv.astype(BF16)
    sks_ref[...] = sk
    svs_ref[...] = sv
    sk_t = sk.T.reshape(SB_HEADS, SB_DH, TOK_TILE)
    sv_t = sv.T.reshape(SB_HEADS, SB_DH, TOK_TILE)
    lr = _dot(h2, wcat_ref[:, ZG_W + 3 * SB_W:])
    logit = jnp.dot(lr, gw_ref[...], precision=lax.Precision.HIGHEST,
                    preferred_element_type=F32) + gb_ref[...]
    la_ref[...] = (jnp.minimum(logit, 0.0) - jnp.log1p(jnp.exp(-jnp.abs(logit)))) * (1.0 / GLA_GATE_TEMP)
    zg_ref[...] = _dot(h2, wcat_ref[:, 0:ZG_W])

    @pl.when(is_prompt)
    def _():
        skt_ref[0] = sk_t
        svt_ref[0] = sv_t


def _ffn_in(x_p, x_s, batch, pre, w_in_b, w_out_b, post, mixpre, wcat_b, gw_pad, gb):
    n_p, n_s = x_p.shape[0], x_s.shape[0]
    n = n_p + n_s
    seq = n_p // batch
    n_pt, per_seq = n_p // TOK_TILE, seq // TOK_TILE
    row = lambda i: (i, 0)
    p_row = lambda i: (jnp.minimum(i, n_pt - 1), 0)
    s_row = lambda i: (jnp.maximum(i - n_pt, 0), 0)
    p_kv = lambda i: (jnp.minimum(i, n_pt - 1) // per_seq, 0, 0, jnp.minimum(i, n_pt - 1) % per_seq)
    vecd = _resident((1, D_MODEL))
    out_shapes = (
        jax.ShapeDtypeStruct((n, D_MODEL), F32),
        jax.ShapeDtypeStruct((n, ZG_W), F32),
        jax.ShapeDtypeStruct((n, GLA_QK_W), F32),
        jax.ShapeDtypeStruct((n, SB_W), BF16),
        jax.ShapeDtypeStruct((n, SB_W), BF16),
        jax.ShapeDtypeStruct((n, SB_W), BF16),
        jax.ShapeDtypeStruct((batch, SB_HEADS, SB_DH, seq), F32),
        jax.ShapeDtypeStruct((batch, SB_HEADS, SB_DH, seq), F32),
        jax.ShapeDtypeStruct((n_s, SB_W), F32),
        jax.ShapeDtypeStruct((n_s, SB_W), F32),
    )
    return pl.pallas_call(
        functools.partial(_ffn_in_body, n_pt),
        grid=(n // TOK_TILE,),
        in_specs=[
            pl.BlockSpec((TOK_TILE, D_MODEL), p_row),
            pl.BlockSpec((TOK_TILE, D_MODEL), s_row),
            vecd,
            _resident(w_in_b.shape),
            _resident(w_out_b.shape),
            vecd, vecd,
            _resident(wcat_b.shape),
            _resident(gw_pad.shape),
            _resident((1, GLA_QK_W)),
        ],
        out_specs=[
            pl.BlockSpec((TOK_TILE, D_MODEL), row),
            pl.BlockSpec((TOK_TILE, ZG_W), row),
            pl.BlockSpec((TOK_TILE, GLA_QK_W), row),
            pl.BlockSpec((TOK_TILE, SB_W), row),
            pl.BlockSpec((TOK_TILE, SB_W), row),
            pl.BlockSpec((TOK_TILE, SB_W), row),
            pl.BlockSpec((1, SB_HEADS, SB_DH, TOK_TILE), p_kv),
            pl.BlockSpec((1, SB_HEADS, SB_DH, TOK_TILE), p_kv),
            pl.BlockSpec((TOK_TILE, SB_W), s_row),
            pl.BlockSpec((TOK_TILE, SB_W), s_row),
        ],
        out_shape=out_shapes,
        scratch_shapes=[pltpu.VMEM((TOK_TILE, D_MODEL), F32)],
        compiler_params=pltpu.CompilerParams(dimension_semantics=("arbitrary",),
                                             vmem_limit_bytes=VMEM_LIMIT),
        name="ffn_in",
    )(x_p, x_s, pre, w_in_b, w_out_b, post, mixpre, wcat_b, gw_pad, gb)


def _merge_out_body(n_ptiles, x1_ref, ap_ref, as_ref, sp_ref, ss_ref, pp_ref, ps_ref, wout_ref, mixpost_ref,
                    pre_ref, win_ref, woutf_ref, post_ref, plepre_ref, pleg_ref, pleu_ref, plepost_ref,
                    yp_ref, ys_ref, acc_ref):
    is_prompt = pl.program_id(0) < n_ptiles
    a = jnp.where(is_prompt, ap_ref[...], as_ref[...])
    s = jnp.where(is_prompt, sp_ref[...], ss_ref[...])
    p = jnp.where(is_prompt, pp_ref[...], ps_ref[...])
    m = _dot(a, wout_ref[0:GLA_V_W, :]) + _dot(s, wout_ref[GLA_V_W:, :])
    x2 = x1_ref[...] + _rms(m, mixpost_ref[...])
    _swiglu_ffn(_rms(x2, pre_ref[...]).astype(BF16), win_ref, woutf_ref, acc_ref)
    x3 = x2 + 0.5 * _rms(acc_ref[...], post_ref[...])
    hg = _rms(x3, plepre_ref[...]).astype(BF16)
    gate = _sigmoid(_dot(hg, pleg_ref[...]))
    e = _dot(p.astype(BF16), pleu_ref[...])
    y = x3 + _rms(gate * e, plepost_ref[...])

    @pl.when(is_prompt)
    def _():
        yp_ref[...] = y

    @pl.when(jnp.logical_not(is_prompt))
    def _():
        ys_ref[...] = y


def _merge_out(x1, a_p, a_s, s_p, s_s, p_p, p_s, wout_b, mixpost, pre, w_in_b, w_out_b, post, plepre, pleg_b,
               pleu_b, plepost):
    n_p, n_s = a_p.shape[0], a_s.shape[0]
    n_pt = n_p // TOK_TILE
    row = lambda i: (i, 0)
    p_row = lambda i: (jnp.minimum(i, n_pt - 1), 0)
    s_row = lambda i: (jnp.maximum(i - n_pt, 0), 0)
    pair = lambda width: [pl.BlockSpec((TOK_TILE, width), p_row), pl.BlockSpec((TOK_TILE, width), s_row)]
    vecd = _resident((1, D_MODEL))
    return pl.pallas_call(
        functools.partial(_merge_out_body, n_pt),
        grid=((n_p + n_s) // TOK_TILE,),
        in_specs=[
            pl.BlockSpec((TOK_TILE, D_MODEL), row),
            *pair(GLA_V_W), *pair(SB_W), *pair(PLE_DIM),
            _resident(wout_b.shape),
            vecd, vecd,
            _resident(w_in_b.shape),
            _resident(w_out_b.shape),
            vecd, vecd,
            _resident(pleg_b.shape),
            _resident(pleu_b.shape),
            vecd,
        ],
        out_specs=pair(D_MODEL),
        out_shape=(jax.ShapeDtypeStruct((n_p, D_MODEL), F32), jax.ShapeDtypeStruct((n_s, D_MODEL), F32)),
        scratch_shapes=[pltpu.VMEM((TOK_TILE, D_MODEL), F32)],
        compiler_params=pltpu.CompilerParams(dimension_semantics=("arbitrary",),
                                             vmem_limit_bytes=VMEM_LIMIT),
        name="merge_out",
    )(x1, a_p, a_s, s_p, s_s, p_p, p_s, wout_b, mixpost, pre, w_in_b, w_out_b, post, plepre, pleg_b, pleu_b,
      plepost)


GLA_BLOCK = 256
GLA_PAIR = 128
GLA_PAR = 2


def _cum_matrices(block, chunk, sub=None):
    t = np.arange(block)[:, None]
    s = np.arange(block)[None, :]
    same = (t // chunk) == (s // chunk)
    mats = [same & (s <= t)]
    if sub is not None:
        mats.append(same & ((s % chunk) < sub * ((t % chunk) // sub)))
    mats.append(same)
    return np.concatenate(mats, axis=0).astype(np.float32)


def _gla_seq_block(bb, zg_ref, la_ref, norm_ref, cum_ref, a_ref, st_ref):
    nb = GLA_BLOCK
    cums = _dot01(cum_ref[...], la_ref[...])
    b = cums[0:nb]
    r = cums[nb:2 * nb]
    bl = cums[2 * nb:3 * nb]
    q = zg_ref[:, 0:GLA_QK_W] * GLA_DK ** -0.5
    k = zg_ref[:, GLA_QK_W:2 * GLA_QK_W]
    q_intra = q * jnp.exp(b - r)
    q_inter = q * jnp.exp(b)
    k_state = (k * jnp.exp(bl - b)).astype(BF16)
    decay = jnp.exp(bl)

    C, S = GLA_CHUNK, GLA_SUB
    n_sub = C // S
    hc = GLA_HEADS * C
    head_sel = _div(_iota((hc, GLA_QK_W), 1), GLA_DK) == _div(_iota((hc, GLA_QK_W), 0), C)
    tt = _mod(_iota((hc, n_sub * C), 0), C)
    cc = _iota((hc, n_sub * C), 1)
    att_keep = (_div(cc, C) == _div(tt, S)) & (_mod(cc, C) <= tt)
    srow = _iota((C, GLA_QK_W), 0)
    pair_chunk = _div(_iota((GLA_DV, GLA_PAIR), 1), C)
    st_lane_head = _div(_iota((GLA_DV, GLA_QK_W), 1), GLA_DK)

    for c in range(nb // C):
        rows = slice(c * C, (c + 1) * C)
        pair = slice((c // 2) * GLA_PAIR, (c // 2 + 1) * GLA_PAIR)
        b_c, k_c = b[rows], k[rows]
        kst = []
        for sidx in range(n_sub):
            r_row = r[c * C + sidx * S:c * C + sidx * S + 1, :]
            kk = k_c * jnp.exp(jnp.minimum(r_row - b_c, EXP_CLAMP))
            kst.append(jnp.where(srow < (sidx + 1) * S, kk, 0.0))
        kst = jnp.concatenate(kst, axis=0).astype(BF16)
        qst = jnp.where(head_sel, jnp.concatenate([q_intra[rows]] * GLA_HEADS, axis=0), 0.0).astype(BF16)
        att = jnp.where(att_keep, _dot_nt(qst, kst), 0.0).astype(BF16)
        qin = jnp.where(head_sel, jnp.concatenate([q_inter[rows]] * GLA_HEADS, axis=0), 0.0).astype(BF16)
        o_inter = _dot_nt(qin, st_ref[bb].astype(BF16))
        upd = jnp.zeros((GLA_DV, GLA_QK_W), F32)
        for h in range(GLA_HEADS):
            vcol = slice(2 * GLA_QK_W + h * GLA_DV, 2 * GLA_QK_W + (h + 1) * GLA_DV)
            gcol = slice(2 * GLA_QK_W + GLA_V_W + h * GLA_DV, 2 * GLA_QK_W + GLA_V_W + (h + 1) * GLA_DV)
            v_h = zg_ref[rows, vcol].astype(BF16)
            o = _dot(att[h * C:(h + 1) * C], jnp.concatenate([v_h] * n_sub, axis=0)) + o_inter[h * C:(h + 1) * C]
            gate = zg_ref[rows, gcol]
            a_ref[bb, rows, h * GLA_DV:(h + 1) * GLA_DV] = (
                _rms(o, norm_ref[...]) * (gate * _sigmoid(gate))).astype(BF16)
            v_t = zg_ref[pair, vcol].T
            u_t = _dot(jnp.where(pair_chunk == (c % 2), v_t, 0.0).astype(BF16), k_state[pair])
            upd = upd + jnp.where(st_lane_head == h, u_t, 0.0)
        st_ref[bb] = st_ref[bb] * decay[c * C:c * C + 1, :] + upd


def _gla_prompt_body(n_par, *refs):
    zg_refs, la_refs = refs[:n_par], refs[n_par:2 * n_par]
    norm_ref, cum_ref, a_ref, sout_ref, st_ref = refs[2 * n_par:]
    i = pl.program_id(1)

    @pl.when(i == 0)
    def _():
        st_ref[...] = jnp.zeros_like(st_ref)

    for bb in range(n_par):
        _gla_seq_block(bb, zg_refs[bb], la_refs[bb], norm_ref, cum_ref, a_ref, st_ref)

    @pl.when(i == pl.num_programs(1) - 1)
    def _():
        for bb in range(n_par):
            sout_ref[bb] = st_ref[bb].T.reshape(GLA_HEADS, GLA_DK, GLA_DV)


def _gla_prompt(zg, la, norm, batch, seq):
    n_par = GLA_PAR if batch % GLA_PAR == 0 else 1
    nblk = seq // GLA_BLOCK
    cum = jnp.asarray(_cum_matrices(GLA_BLOCK, GLA_CHUNK, GLA_SUB), BF16)
    rows = [lambda g, i, bb=bb: ((g * n_par + bb) * nblk + i, 0) for bb in range(n_par)]
    return pl.pallas_call(
        functools.partial(_gla_prompt_body, n_par),
        grid=(batch // n_par, nblk),
        in_specs=[
            *[pl.BlockSpec((GLA_BLOCK, ZG_W), r) for r in rows],
            *[pl.BlockSpec((GLA_BLOCK, GLA_QK_W), r) for r in rows],
            pl.BlockSpec((1, GLA_DV), lambda g, i: (0, 0)),
            pl.BlockSpec(cum.shape, lambda g, i: (0, 0)),
        ],
        out_specs=[
            pl.BlockSpec((n_par, GLA_BLOCK, GLA_V_W), lambda g, i: (g, i, 0)),
            pl.BlockSpec((n_par, GLA_HEADS, GLA_DK, GLA_DV), lambda g, i: (g, 0, 0, 0)),
        ],
        out_shape=(jax.ShapeDtypeStruct((batch, seq, GLA_V_W), BF16),
                   jax.ShapeDtypeStruct((batch, GLA_HEADS, GLA_DK, GLA_DV), F32)),
        scratch_shapes=[pltpu.VMEM((n_par, GLA_DV, GLA_QK_W), F32)],
        compiler_params=pltpu.CompilerParams(dimension_semantics=("arbitrary", "arbitrary"),
                                             vmem_limit_bytes=VMEM_LIMIT),
        name="gla_prompt",
    )(*([zg] * n_par), *([la] * n_par), norm, cum)


GLA_S_ROWS = 128


def _gla_sample_body(n_new, zg_ref, la_ref, s_ref, norm_ref, cum_ref, a_ref, sout_ref,
                     qin_ref, khat_ref, dec_ref, oint_ref, vt_ref):
    rows = GLA_S_ROWS
    cums = _dot01(cum_ref[...], la_ref[...])
    b = cums[0:rows]
    bl = cums[rows:2 * rows]
    q = zg_ref[:, 0:GLA_QK_W] * GLA_DK ** -0.5
    k = zg_ref[:, GLA_QK_W:2 * GLA_QK_W]
    q_dec = q * jnp.exp(b)
    k_intra = (k * jnp.exp(jnp.minimum(-b, EXP_CLAMP))).astype(BF16)
    qin_ref[...] = q_dec
    khat_ref[...] = (k * jnp.exp(bl - b)).astype(BF16)
    dec_ref[...] = jnp.exp(bl)

    hr = GLA_HEADS * rows
    head_sel = _div(_iota((hr, GLA_QK_W), 1), GLA_DK) == _div(_iota((hr, GLA_QK_W), 0), rows)
    qst = jnp.where(head_sel, jnp.concatenate([q_dec] * GLA_HEADS, axis=0), 0.0).astype(BF16)
    tt = _mod(_iota((hr, rows), 0), rows)
    ss = _iota((hr, rows), 1)
    keep = (_div(tt, n_new) == _div(ss, n_new)) & (ss <= tt)
    att = jnp.where(keep, _dot_nt(qst, k_intra), 0.0).astype(BF16)
    for h in range(GLA_HEADS):
        vcol = slice(2 * GLA_QK_W + h * GLA_DV, 2 * GLA_QK_W + (h + 1) * GLA_DV)
        v_h = zg_ref[:, vcol]
        oint_ref[:, h * GLA_DV:(h + 1) * GLA_DV] = _dot(att[h * rows:(h + 1) * rows], v_h.astype(BF16))
        vt_ref[h] = v_h.T

    per8 = SUBLANES // n_new
    sub8 = _div(_iota((SUBLANES, GLA_QK_W), 0), n_new)
    lane8 = _div(_iota((SUBLANES, GLA_QK_W), 1), GLA_DK)
    out8 = _div(_iota((SUBLANES, GLA_DV), 0), n_new)
    vt_seq = _div(_iota((GLA_DV, rows), 1), n_new)
    st_lane_head = _div(_iota((GLA_DV, GLA_QK_W), 1), GLA_DK)

    def group_step(m, carry):
        r8 = pl.ds(pl.multiple_of(m * SUBLANES, SUBLANES), SUBLANES)
        q8 = qin_ref[r8, :]
        d8 = dec_ref[r8, :]
        res = []
        for e in range(per8):
            n = m * per8 + e
            s_n = s_ref[0, n].reshape(GLA_HEADS * GLA_DK, GLA_DV)
            lhs = jnp.concatenate([jnp.where((sub8 == e) & (lane8 == h), q8, 0.0) for h in range(GLA_HEADS)],
                                  axis=0)
            res.append(_dot(lhs.astype(BF16), s_n.astype(BF16)))
            upd = jnp.zeros((GLA_DV, GLA_QK_W), F32)
            for h in range(GLA_HEADS):
                u_t = _dot(jnp.where(vt_seq == n, vt_ref[h], 0.0).astype(BF16), khat_ref[...])
                upd = upd + jnp.where(st_lane_head == h, u_t, 0.0)
            s_new_t = s_n.T * d8[e * n_new:e * n_new + 1, :] + upd
            sout_ref[0, n] = s_new_t.T.reshape(GLA_HEADS, GLA_DK, GLA_DV)
        for h in range(GLA_HEADS):
            o8 = res[0][h * SUBLANES:(h + 1) * SUBLANES]
            for e in range(1, per8):
                o8 = jnp.where(out8 == e, res[e][h * SUBLANES:(h + 1) * SUBLANES], o8)
            oint_ref[r8, h * GLA_DV:(h + 1) * GLA_DV] += o8
        return carry

    lax.fori_loop(0, rows // SUBLANES, group_step, 0)

    for h in range(GLA_HEADS):
        gcol = slice(2 * GLA_QK_W + GLA_V_W + h * GLA_DV, 2 * GLA_QK_W + GLA_V_W + (h + 1) * GLA_DV)
        gate = zg_ref[:, gcol]
        o = oint_ref[:, h * GLA_DV:(h + 1) * GLA_DV]
        a_ref[:, h * GLA_DV:(h + 1) * GLA_DV] = (_rms(o, norm_ref[...]) * (gate * _sigmoid(gate))).astype(BF16)


def _gla_sample(zg, la, state, norm, row0, n_seq, n_new):
    assert SUBLANES % n_new == 0 and GLA_S_ROWS % n_new == 0
    seqs = GLA_S_ROWS // n_new
    steps = n_seq // seqs
    blk0 = row0 // GLA_S_ROWS
    cum = jnp.asarray(_cum_matrices(GLA_S_ROWS, n_new), BF16)
    return pl.pallas_call(
        functools.partial(_gla_sample_body, n_new),
        grid=(steps,),
        in_specs=[
            pl.BlockSpec((GLA_S_ROWS, ZG_W), lambda i: (blk0 + i, 0)),
            pl.BlockSpec((GLA_S_ROWS, GLA_QK_W), lambda i: (blk0 + i, 0)),
            pl.BlockSpec((1, seqs, GLA_HEADS, GLA_DK, GLA_DV), lambda i: (0, i, 0, 0, 0)),
            pl.BlockSpec((1, GLA_DV), lambda i: (0, 0)),
            pl.BlockSpec(cum.shape, lambda i: (0, 0)),
        ],
        out_specs=[
            pl.BlockSpec((GLA_S_ROWS, GLA_V_W), lambda i: (i, 0)),
            pl.BlockSpec((1, seqs, GLA_HEADS, GLA_DK, GLA_DV), lambda i: (0, i, 0, 0, 0)),
        ],
        out_shape=(jax.ShapeDtypeStruct((n_seq * n_new, GLA_V_W), BF16),
                   jax.ShapeDtypeStruct((1, n_seq, GLA_HEADS, GLA_DK, GLA_DV), F32)),
        scratch_shapes=[
            pltpu.VMEM((GLA_S_ROWS, GLA_QK_W), F32),
            pltpu.VMEM((GLA_S_ROWS, GLA_QK_W), BF16),
            pltpu.VMEM((GLA_S_ROWS, GLA_QK_W), F32),
            pltpu.VMEM((GLA_S_ROWS, GLA_V_W), F32),
            pltpu.VMEM((GLA_HEADS, GLA_DV, GLA_S_ROWS), F32),
        ],
        compiler_params=pltpu.CompilerParams(dimension_semantics=("arbitrary",),
                                             vmem_limit_bytes=VMEM_LIMIT),
        name="gla_sample",
    )(zg, la, state, norm, cum)


SB_TILE = 256
SB_PAIR = LANES // SB_DH
SB_HPS = 4
SB_BLK = SB_HPS * SB_DH
SB_GROUP = 2
SB_MASKED = -1e30


def _sb_prompt_body(bias_ref, q_ref, k_ref, v_ref, norm_ref, u_ref, o_ref, acc_ref, carry_ref, bias_buf, zs_buf,
                    cs_buf, tot_buf):
    p = pl.program_id(1)
    i = pl.program_id(2)
    nq = pl.num_programs(2) - 1
    qi = jnp.minimum(i, nq - 1)
    qprev = jnp.maximum(i - 1, 0)
    T = SB_TILE
    lane_head = _div(_iota((T, SB_BLK), 1), SB_DH)
    q = q_ref[...]
    qh = [jnp.where(lane_head == e, q, jnp.zeros_like(q)) for e in range(SB_HPS)]
    u = u_ref[...]

    @pl.when(i == 0)
    def _():
        valid = _iota((T, T), 1) < _iota((T, T), 0)
        for e in range(SB_HPS):
            b = bias_ref[p * SB_HPS + e]
            bias_buf[e, 0] = jnp.where(valid, b, SB_MASKED)
            bias_buf[e, 1] = jnp.full((T, T), b, F32)
            bias_buf[e, 2] = jnp.full((T, T), SB_MASKED, F32)
        acc_ref[...] = jnp.zeros_like(acc_ref)
        carry_ref[...] = jnp.zeros_like(carry_ref)
        zs_buf[...] = jnp.zeros_like(zs_buf)
        cs_buf[...] = jnp.zeros_like(cs_buf)
        tot_buf[...] = jnp.zeros_like(tot_buf)

    def groups_of(qt):
        return _div(qt + SB_GROUP, SB_GROUP)

    def tile_of(qt, t):
        kj = qt - t
        kind = jnp.where(t == 0, 0, jnp.where(kj >= 0, 1, 2))
        return kind, pl.ds(pl.multiple_of(jnp.maximum(kj, 0) * T, T), T)

    def logits_stage(g):
        for j in range(SB_GROUP):
            kind, rows = tile_of(qi, g * SB_GROUP + j)
            kt = k_ref[rows, :]
            for e in range(SB_HPS):
                z = _dot_nt(qh[e], kt) + bias_buf[e, kind]
                sp = _softplus2(z)
                zs_buf[j, e] = z - sp
                cs = _dot(sp.astype(BF16), u)
                cs_buf[j, e] = cs
                tot_buf[j, e] = cs[:, 0:1] + sp[:, 0:1]

    def values_stage(qt, g):
        for j in range(SB_GROUP):
            _, rows = tile_of(qt, g * SB_GROUP + j)
            vt = v_ref[rows, :]
            for e in range(SB_HPS):
                w = jnp.exp2(zs_buf[j, e] - cs_buf[j, e])
                own = slice((e // SB_PAIR) * LANES, (e // SB_PAIR + 1) * LANES)
                acc_ref[e] += jnp.exp2(-carry_ref[e]) * _dot(w.astype(BF16), vt)[:, own]
                carry_ref[e] += tot_buf[j, e]

    values_stage(qprev, groups_of(qprev) - 1)

    half_head = _div(_iota((T, LANES), 1), SB_DH)
    for blk in range(SB_HPS // SB_PAIR):
        o = acc_ref[blk * SB_PAIR]
        for e in range(1, SB_PAIR):
            o = jnp.where(half_head == e, acc_ref[blk * SB_PAIR + e], o)
        sq = o * o
        ms = jnp.zeros_like(o)
        for e in range(SB_PAIR):
            ms_e = jnp.sum(jnp.where(half_head == e, sq, 0.0), axis=1, keepdims=True) * (1.0 / SB_DH)
            ms = jnp.where(half_head == e, ms_e, ms)
        lanes = slice(blk * LANES, (blk + 1) * LANES)
        o_ref[:, lanes] = (o * lax.rsqrt(ms + EPS) * norm_ref[:, lanes]).astype(BF16)

    acc_ref[...] = jnp.zeros_like(acc_ref)
    carry_ref[...] = jnp.zeros_like(carry_ref)
    logits_stage(0)

    def step(g, c):
        values_stage(qi, g - 1)
        logits_stage(g)
        return c

    lax.fori_loop(1, jnp.where(i < nq, groups_of(qi), 1), step, 0)


def _sb_prompt(bias, sq_b, sk_b, sv_b, norm_blk, batch, seq):
    nq = seq // SB_TILE
    nblk = SB_HEADS // SB_HPS
    j = np.arange(SB_TILE)[:, None]
    s = np.arange(SB_TILE)[None, :]
    u = jnp.asarray((j > s).astype(np.float32), BF16)
    return pl.pallas_call(
        _sb_prompt_body,
        grid=(batch, nblk, nq + 1),
        in_specs=[
            pl.BlockSpec(memory_space=pltpu.SMEM),
            pl.BlockSpec((SB_TILE, SB_BLK), lambda b, p, i: (b * nq + jnp.minimum(i, nq - 1), p)),
            pl.BlockSpec((seq, SB_BLK), lambda b, p, i: (b, p)),
            pl.BlockSpec((seq, SB_BLK), lambda b, p, i: (b, p)),
            pl.BlockSpec((1, SB_BLK), lambda b, p, i: (0, 0)),
            pl.BlockSpec((SB_TILE, SB_TILE), lambda b, p, i: (0, 0)),
        ],
        out_specs=pl.BlockSpec((SB_TILE, SB_BLK), lambda b, p, i: (b * nq + jnp.maximum(i - 1, 0), p)),
        out_shape=jax.ShapeDtypeStruct((batch * seq, SB_W), BF16),
        scratch_shapes=[
            pltpu.VMEM((SB_HPS, SB_TILE, LANES), F32),
            pltpu.VMEM((SB_HPS, SB_TILE, 1), F32),
            pltpu.VMEM((SB_HPS, 3, SB_TILE, SB_TILE), F32),
            pltpu.VMEM((SB_GROUP, SB_HPS, SB_TILE, SB_TILE), F32),
            pltpu.VMEM((SB_GROUP, SB_HPS, SB_TILE, SB_TILE), F32),
            pltpu.VMEM((SB_GROUP, SB_HPS, SB_TILE, 1), F32),
        ],
        compiler_params=pltpu.CompilerParams(dimension_semantics=("arbitrary", "arbitrary", "arbitrary"),
                                             vmem_limit_bytes=VMEM_LIMIT),
        name="sb_prompt",
    )(bias, sq_b, sk_b, sv_b, norm_blk, u)


def _sb_scan_matrix():
    s_in = np.arange(LANES)[:, None]
    s_out = np.arange(LANES)[None, :]
    return np.concatenate([s_in > s_out, np.ones((LANES, LANES), bool)], axis=1).astype(np.float32)


def _sb_sample_body(n_new, n_pages, pt_ref, bias_ref, q_ref, kn_ref, vn_ref, *rest):
    del pt_ref
    k_refs, v_refs = rest[:n_pages], rest[n_pages:2 * n_pages]
    norm_ref, up_ref, o_ref = rest[2 * n_pages:]
    qrows = n_new * SB_HEADS
    own_head = _mod(_iota((qrows, SB_W), 0), SB_HEADS) == _div(_iota((qrows, SB_W), 1), SB_DH)
    q_rows = jnp.concatenate([jnp.broadcast_to(q_ref[0, t:t + 1, :], (SB_HEADS, SB_W)) for t in range(n_new)],
                             axis=0)
    qbd = jnp.where(own_head, q_rows, 0.0).astype(BF16)
    bias = bias_ref[...]

    zs = [_dot(qbd, k_refs[g][...].reshape(SB_W, PAGE_SIZE).astype(BF16)) + bias for g in range(n_pages)]
    pad = jnp.zeros((LANES - SUBLANES, SB_W), F32)
    kn = jnp.concatenate([kn_ref[0], pad], axis=0).astype(BF16)
    vn = jnp.concatenate([vn_ref[0], pad], axis=0).astype(BF16)
    zs.append(_dot_nt(qbd, kn) + bias)
    new_ok = _iota((qrows, LANES), 1) < _div(_iota((qrows, LANES), 0), SB_HEADS)
    sps = [_softplus2(z) for z in zs]
    sps[n_pages] = jnp.where(new_ok, sps[n_pages], 0.0)
    r = _dot(jnp.concatenate(sps, axis=0).astype(BF16), up_ref[...])
    run = jnp.zeros((qrows, LANES), F32)
    acc = jnp.zeros((qrows, SB_W), F32)
    for g in reversed(range(n_pages + 1)):
        blk = r[g * qrows:(g + 1) * qrows]
        w = jnp.exp2(zs[g] - sps[g] - blk[:, 0:LANES] - run)
        run = run + blk[:, LANES:2 * LANES]
        if g == n_pages:
            acc = acc + _dot(jnp.where(new_ok, w, 0.0).astype(BF16), vn)
        else:
            acc = acc + _dot_nt(w.astype(BF16), v_refs[g][...].reshape(SB_W, PAGE_SIZE).astype(BF16))
    acc = jnp.where(own_head, acc, 0.0)
    o = acc[:, 0:SB_DH]
    for h in range(1, SB_HEADS):
        o = o + acc[:, h * SB_DH:(h + 1) * SB_DH]
    o_ref[0] = _rms(o, norm_ref[...]).astype(BF16)


def _sb_sample(page_table, bias_rows, q_s, k_new, v_new, pool_kt, pool_vt, norm, layer, n_new):
    n_seq, n_pages = page_table.shape
    qrows = n_new * SB_HEADS
    assert PAGE_SIZE == LANES and n_new <= SUBLANES
    up = jnp.asarray(_sb_scan_matrix(), BF16)
    seq3 = lambda b, pt: (b, 0, 0)
    const = lambda b, pt: (0, 0)
    page_specs = [pl.BlockSpec((None, None, SB_HEADS, SB_DH, PAGE_SIZE),
                               lambda b, pt, g=g: (layer, pt[b * n_pages + g], 0, 0, 0)) for g in range(n_pages)]
    grid_spec = pltpu.PrefetchScalarGridSpec(
        num_scalar_prefetch=1,
        grid=(n_seq,),
        in_specs=[
            pl.BlockSpec(bias_rows.shape, const),
            pl.BlockSpec((1, SUBLANES, SB_W), seq3),
            pl.BlockSpec((1, SUBLANES, SB_W), seq3),
            pl.BlockSpec((1, SUBLANES, SB_W), seq3),
            *page_specs, *page_specs,
            pl.BlockSpec((1, SB_DH), const),
            pl.BlockSpec(up.shape, const),
        ],
        out_specs=pl.BlockSpec((1, qrows, SB_DH), seq3),
    )
    return pl.pallas_call(
        functools.partial(_sb_sample_body, n_new, n_pages),
        grid_spec=grid_spec,
        out_shape=jax.ShapeDtypeStruct((n_seq, qrows, SB_DH), BF16),
        compiler_params=pltpu.CompilerParams(dimension_semantics=("arbitrary",),
                                             vmem_limit_bytes=VMEM_LIMIT),
        name="sb_sample",
    )(page_table.reshape(-1), bias_rows, q_s, k_new, v_new, *([pool_kt] * n_pages), *([pool_vt] * n_pages),
      norm, up)


def kernel(x_prompt, x_sample, cache_sb_k, cache_sb_v, state_gla, page_table, p_prompt, p_sample, f1_pre, f1_in, f1_out, f1_post, mix_pre, w_in, gate_w, gate_b, gla_norm, sb_norm, sb_bias, w_out, mix_post, f2_pre, f2_in, f2_out, f2_post, ple_pre, ple_w_gate, ple_w_up, ple_post):
    batch, seq, _ = x_prompt.shape
    n_seq, n_new, _ = x_sample.shape
    depth = f1_in.shape[0]
    n_p, n_s = batch * seq, n_seq * n_new
    assert n_p % TOK_TILE == 0 and n_s % TOK_TILE == 0 and seq % SB_TILE == 0 and n_p % GLA_S_ROWS == 0

    assert seq % TOK_TILE == 0
    x_p, x_s = x_prompt.reshape(n_p, D_MODEL), x_sample.reshape(n_s, D_MODEL)
    vec = lambda a: a.reshape(1, -1)
    lr0 = ZG_W
    sb0 = ZG_W + GLA_RANK
    pk, pv, ps, sk_l, sv_l, ss = [], [], [], [], [], []
    for l in range(depth):
        wcat = jnp.concatenate(
            [w_in[l][:, :lr0], w_in[l][:, sb0:], w_in[l][:, lr0:sb0],
             jnp.zeros((D_MODEL, LANES - GLA_RANK), F32)], axis=1).astype(BF16)
        gw_pad = jnp.concatenate([gate_w[l], jnp.zeros((LANES - GLA_RANK, GLA_QK_W), F32)], axis=0)
        x1, zg, la, sq_b, sk_b, sv_b, sk_t, sv_t, sk_s, sv_s = _ffn_in(
            x_p, x_s, batch, vec(f1_pre[l]), f1_in[l].astype(BF16), f1_out[l].astype(BF16), vec(f1_post[l]),
            vec(mix_pre[l]), wcat, gw_pad, vec(gate_b[l]))

        a_p, s_gla_p = _gla_prompt(zg, la, vec(gla_norm[l]), batch, seq)
        a_p = a_p.reshape(n_p, GLA_V_W)
        bias2 = sb_bias[l] * LOG2E
        o_sb_p = _sb_prompt(bias2, sq_b, sk_b, sv_b, vec(jnp.tile(sb_norm[l], SB_HPS)), batch, seq)

        a_s, s_gla_s = _gla_sample(zg, la, state_gla[l:l + 1], vec(gla_norm[l]), n_p, n_seq, n_new)
        rows8 = lambda a: jnp.pad(a.astype(F32).reshape(n_seq, n_new, SB_W),
                                  ((0, 0), (0, SUBLANES - n_new), (0, 0)))
        bias_rows = jnp.broadcast_to(jnp.tile(bias2, n_new)[:, None], (n_new * SB_HEADS, LANES))
        to_hdk = lambda c: jnp.transpose(c, (0, 1, 3, 4, 2))
        o_sb_s = _sb_sample(page_table, bias_rows, rows8(sq_b[n_p:]), rows8(sk_s), rows8(sv_s),
                            to_hdk(cache_sb_k), to_hdk(cache_sb_v), vec(sb_norm[l]), l, n_new)

        x_p, x_s = _merge_out(
            x1, a_p, a_s, o_sb_p, o_sb_s.reshape(n_s, SB_W), p_prompt[l].reshape(n_p, PLE_DIM),
            p_sample[l].reshape(n_s, PLE_DIM), w_out[l].astype(BF16), vec(mix_post[l]), vec(f2_pre[l]),
            f2_in[l].astype(BF16), f2_out[l].astype(BF16), vec(f2_post[l]), vec(ple_pre[l]),
            ple_w_gate[l].astype(BF16), ple_w_up[l].astype(BF16), vec(ple_post[l]))

        pk.append(jnp.transpose(sk_t, (0, 3, 1, 2)))
        pv.append(jnp.transpose(sv_t, (0, 3, 1, 2)))
        ps.append(s_gla_p)
        sk_l.append(sk_s.reshape(n_seq, n_new, SB_HEADS, SB_DH))
        sv_l.append(sv_s.reshape(n_seq, n_new, SB_HEADS, SB_DH))
        ss.append(s_gla_s[0])
    return (x_p.reshape(batch, seq, D_MODEL), x_s.reshape(n_seq, n_new, D_MODEL),
            jnp.stack(pk), jnp.stack(pv), jnp.stack(ps), jnp.stack(sk_l), jnp.stack(sv_l), jnp.stack(ss))
```

```python
import functools

import numpy as np
import jax
import jax.numpy as jnp
from jax import lax
from jax.experimental import pallas as pl
from jax.experimental.pallas import tpu as pltpu

F32 = jnp.float32
BF16 = jnp.bfloat16

D_MODEL = 1024
D_FF = 2816
PLE_DIM = 256
GLA_HEADS = 4
GLA_DK = 64
GLA_DV = 128
GLA_RANK = 16
GLA_GATE_TEMP = 16.0
GLA_CHUNK = 64
GLA_SUB = 16
SB_HEADS = 8
SB_DH = 64
PAGE_SIZE = 128
EPS = 1e-6
GLA_QK_W = GLA_HEADS * GLA_DK
GLA_V_W = GLA_HEADS * GLA_DV
SB_W = SB_HEADS * SB_DH
ZG_W = 2 * GLA_QK_W + 2 * GLA_V_W
LANES = 128
SUBLANES = 8
EXP_CLAMP = 80.0
LOG2E = 1.4426950408889634

TOK_TILE = 512
FF_TILE = 256
VMEM_LIMIT = 62 * 1024 * 1024


def _log2(n):
    k = int(n).bit_length() - 1
    assert 1 << k == n, n
    return k


def _div(x, n):
    return lax.shift_right_logical(x, _log2(n))


def _mod(x, n):
    assert n & (n - 1) == 0
    return x & (n - 1)


def _iota(shape, dim):
    return lax.broadcasted_iota(jnp.int32, shape, dim)


def _dot(a, b):
    return jnp.dot(a, b, preferred_element_type=F32)


def _dot_nt(a, b):
    return lax.dot_general(a, b, (((1,), (1,)), ((), ())), preferred_element_type=F32)


def _dot01(t_bf16, x):
    hi = x.astype(BF16)
    lo = (x - hi.astype(F32)).astype(BF16)
    return _dot(t_bf16, hi) + _dot(t_bf16, lo)


def _rms(x, w):
    ms = jnp.mean(x * x, axis=-1, keepdims=True)
    return x * lax.rsqrt(ms + EPS) * w


def _sigmoid(x):
    return 1.0 / (1.0 + jnp.exp(-x))


def _softplus2(x):
    return jnp.maximum(x, jnp.log(1.0 + jnp.exp2(jnp.minimum(x, 126.0))) * LOG2E)


def _swiglu_ffn(h, win_ref, wout_ref, acc_ref):
    for j in range(D_FF // FF_TILE):
        g = _dot(h, win_ref[:, j * FF_TILE:(j + 1) * FF_TILE])
        u = _dot(h, win_ref[:, D_FF + j * FF_TILE:D_FF + (j + 1) * FF_TILE])
        act = (g * _sigmoid(g) * u).astype(BF16)
        part = _dot(act, wout_ref[j * FF_TILE:(j + 1) * FF_TILE, :])
        if j == 0:
            acc_ref[...] = part
        else:
            acc_ref[...] += part


def _resident(shape):
    return pl.BlockSpec(shape, lambda i: (0,) * len(shape), pipeline_mode=pl.Buffered(1))


def _ffn_in_body(n_ptiles, xp_ref, xs_ref, pre_ref, win_ref, wout_ref, post_ref, mixpre_ref, wcat_ref, gw_ref,
                 gb_ref, x1_ref, zg_ref, la_ref, sq_ref, skb_ref, svb_ref, skt_ref, svt_ref, sks_ref, svs_ref,
                 acc_ref):
    is_prompt = pl.program_id(0) < n_ptiles
    x = jnp.where(is_prompt, xp_ref[...], xs_ref[...])
    _swiglu_ffn(_rms(x, pre_ref[...]).astype(BF16), win_ref, wout_ref, acc_ref)
    x1 = x + 0.5 * _rms(acc_ref[...], post_ref[...])
    x1_ref[...] = x1
    h2 = _rms(x1, mixpre_ref[...]).astype(BF16)
    sb = _dot(h2, wcat_ref[:, ZG_W:ZG_W + 3 * SB_W])
    sq_ref[...] = (sb[:, 0:SB_W] * (SB_DH ** -0.5 * LOG2E)).astype(BF16)
    sk = sb[:, SB_W:2 * SB_W]
    sv = sb[:, 2 * SB_W:3 * SB_W]
    skb_ref[...] = sk.astype(BF16)
    svb_ref[...] = sv.astype(BF16)
    sks_ref[...] = sk
    svs_ref[...] = sv
    sk_t = sk.T.reshape(SB_HEADS, SB_DH, TOK_TILE)
    sv_t = sv.T.reshape(SB_HEADS, SB_DH, TOK_TILE)
    lr = _dot(h2, wcat_ref[:, ZG_W + 3 * SB_W:])
    logit = jnp.dot(lr, gw_ref[...], precision=lax.Precision.HIGHEST,
                    preferred_element_type=F32) + gb_ref[...]
    la_ref[...] = (jnp.minimum(logit, 0.0) - jnp.log1p(jnp.exp(-jnp.abs(logit)))) * (1.0 / GLA_GATE_TEMP)
    zg_ref[...] = _dot(h2, wcat_ref[:, 0:ZG_W])

    @pl.when(is_prompt)
    def _():
        skt_ref[0] = sk_t
        svt_ref[0] = sv_t


def _ffn_in(x_p, x_s, batch, pre, w_in_b, w_out_b, post, mixpre, wcat_b, gw_pad, gb):
    n_p, n_s = x_p.shape[0], x_s.shape[0]
    n = n_p + n_s
    seq = n_p // batch
    n_pt, per_seq = n_p // TOK_TILE, seq // TOK_TILE
    row = lambda i: (i, 0)
    p_row = lambda i: (jnp.minimum(i, n_pt - 1), 0)
    s_row = lambda i: (jnp.maximum(i - n_pt, 0), 0)
    p_kv = lambda i: (jnp.minimum(i, n_pt - 1) // per_seq, 0, 0, jnp.minimum(i, n_pt - 1) % per_seq)
    vecd = _resident((1, D_MODEL))
    out_shapes = (
        jax.ShapeDtypeStruct((n, D_MODEL), F32),
        jax.ShapeDtypeStruct((n, ZG_W), F32),
        jax.ShapeDtypeStruct((n, GLA_QK_W), F32),
        jax.ShapeDtypeStruct((n, SB_W), BF16),
        jax.ShapeDtypeStruct((n, SB_W), BF16),
        jax.ShapeDtypeStruct((n, SB_W), BF16),
        jax.ShapeDtypeStruct((batch, SB_HEADS, SB_DH, seq), F32),
        jax.ShapeDtypeStruct((batch, SB_HEADS, SB_DH, seq), F32),
        jax.ShapeDtypeStruct((n_s, SB_W), F32),
        jax.ShapeDtypeStruct((n_s, SB_W), F32),
    )
    return pl.pallas_call(
        functools.partial(_ffn_in_body, n_pt),
        grid=(n // TOK_TILE,),
        in_specs=[
            pl.BlockSpec((TOK_TILE, D_MODEL), p_row),
            pl.BlockSpec((TOK_TILE, D_MODEL), s_row),
            vecd,
            _resident(w_in_b.shape),
            _resident(w_out_b.shape),
            vecd, vecd,
            _resident(wcat_b.shape),
            _resident(gw_pad.shape),
            _resident((1, GLA_QK_W)),
        ],
        out_specs=[
            pl.BlockSpec((TOK_TILE, D_MODEL), row),
            pl.BlockSpec((TOK_TILE, ZG_W), row),
            pl.BlockSpec((TOK_TILE, GLA_QK_W), row),
            pl.BlockSpec((TOK_TILE, SB_W), row),
            pl.BlockSpec((TOK_TILE, SB_W), row),
            pl.BlockSpec((TOK_TILE, SB_W), row),
            pl.BlockSpec((1, SB_HEADS, SB_DH, TOK_TILE), p_kv),
            pl.BlockSpec((1, SB_HEADS, SB_DH, TOK_TILE), p_kv),
            pl.BlockSpec((TOK_TILE, SB_W), s_row),
            pl.BlockSpec((TOK_TILE, SB_W), s_row),
        ],
        out_shape=out_shapes,
        scratch_shapes=[pltpu.VMEM((TOK_TILE, D_MODEL), F32)],
        compiler_params=pltpu.CompilerParams(dimension_semantics=("arbitrary",),
                                             vmem_limit_bytes=VMEM_LIMIT),
        name="ffn_in",
    )(x_p, x_s, pre, w_in_b, w_out_b, post, mixpre, wcat_b, gw_pad, gb)


def _merge_out_body(n_ptiles, x1_ref, ap_ref, as_ref, sp_ref, ss_ref, pp_ref, ps_ref, wout_ref, mixpost_ref,
                    pre_ref, win_ref, woutf_ref, post_ref, plepre_ref, pleg_ref, pleu_ref, plepost_ref,
                    yp_ref, ys_ref, acc_ref):
    is_prompt = pl.program_id(0) < n_ptiles
    a = jnp.where(is_prompt, ap_ref[...], as_ref[...])
    s = jnp.where(is_prompt, sp_ref[...], ss_ref[...])
    p = jnp.where(is_prompt, pp_ref[...], ps_ref[...])
    m = _dot(a, wout_ref[0:GLA_V_W, :]) + _dot(s, wout_ref[GLA_V_W:, :])
    x2 = x1_ref[...] + _rms(m, mixpost_ref[...])
    _swiglu_ffn(_rms(x2, pre_ref[...]).astype(BF16), win_ref, woutf_ref, acc_ref)
    x3 = x2 + 0.5 * _rms(acc_ref[...], post_ref[...])
    hg = _rms(x3, plepre_ref[...]).astype(BF16)
    gate = _sigmoid(_dot(hg, pleg_ref[...]))
    e = _dot(p.astype(BF16), pleu_ref[...])
    y = x3 + _rms(gate * e, plepost_ref[...])

    @pl.when(is_prompt)
    def _():
        yp_ref[...] = y

    @pl.when(jnp.logical_not(is_prompt))
    def _():
        ys_ref[...] = y


def _merge_out(x1, a_p, a_s, s_p, s_s, p_p, p_s, wout_b, mixpost, pre, w_in_b, w_out_b, post, plepre, pleg_b,
               pleu_b, plepost):
    n_p, n_s = a_p.shape[0], a_s.shape[0]
    n_pt = n_p // TOK_TILE
    row = lambda i: (i, 0)
    p_row = lambda i: (jnp.minimum(i, n_pt - 1), 0)
    s_row = lambda i: (jnp.maximum(i - n_pt, 0), 0)
    pair = lambda width: [pl.BlockSpec((TOK_TILE, width), p_row), pl.BlockSpec((TOK_TILE, width), s_row)]
    vecd = _resident((1, D_MODEL))
    return pl.pallas_call(
        functools.partial(_merge_out_body, n_pt),
        grid=((n_p + n_s) // TOK_TILE,),
        in_specs=[
            pl.BlockSpec((TOK_TILE, D_MODEL), row),
            *pair(GLA_V_W), *pair(SB_W), *pair(PLE_DIM),
            _resident(wout_b.shape),
            vecd, vecd,
            _resident(w_in_b.shape),
            _resident(w_out_b.shape),
            vecd, vecd,
            _resident(pleg_b.shape),
            _resident(pleu_b.shape),
            vecd,
        ],
        out_specs=pair(D_MODEL),
        out_shape=(jax.ShapeDtypeStruct((n_p, D_MODEL), F32), jax.ShapeDtypeStruct((n_s, D_MODEL), F32)),
        scratch_shapes=[pltpu.VMEM((TOK_TILE, D_MODEL), F32)],
        compiler_params=pltpu.CompilerParams(dimension_semantics=("arbitrary",),
                                             vmem_limit_bytes=VMEM_LIMIT),
        name="merge_out",
    )(x1, a_p, a_s, s_p, s_s, p_p, p_s, wout_b, mixpost, pre, w_in_b, w_out_b, post, plepre, pleg_b, pleu_b,
      plepost)


GLA_BLOCK = 256
GLA_PAIR = 128
GLA_PAR = 4


def _cum_matrices(block, chunk, sub=None):
    t = np.arange(block)[:, None]
    s = np.arange(block)[None, :]
    same = (t // chunk) == (s // chunk)
    mats = [same & (s <= t)]
    if sub is not None:
        mats.append(same & ((s % chunk) < sub * ((t % chunk) // sub)))
    mats.append(same)
    return np.concatenate(mats, axis=0).astype(np.float32)


def _gla_seq_block(bb, zg_ref, la_ref, norm_ref, cum_ref, a_ref, st_ref):
    nb = GLA_BLOCK
    cums = _dot01(cum_ref[...], la_ref[...])
    b = cums[0:nb]
    r = cums[nb:2 * nb]
    bl = cums[2 * nb:3 * nb]
    q = zg_ref[:, 0:GLA_QK_W] * GLA_DK ** -0.5
    k = zg_ref[:, GLA_QK_W:2 * GLA_QK_W]
    q_intra = q * jnp.exp(b - r)
    q_inter = q * jnp.exp(b)
    k_state = (k * jnp.exp(bl - b)).astype(BF16)
    decay = jnp.exp(bl)

    C, S = GLA_CHUNK, GLA_SUB
    n_sub = C // S
    hc = GLA_HEADS * C
    head_sel = _div(_iota((hc, GLA_QK_W), 1), GLA_DK) == _div(_iota((hc, GLA_QK_W), 0), C)
    tt = _mod(_iota((hc, n_sub * C), 0), C)
    cc = _iota((hc, n_sub * C), 1)
    att_keep = (_div(cc, C) == _div(tt, S)) & (_mod(cc, C) <= tt)
    srow = _iota((C, GLA_QK_W), 0)
    pair_chunk = _div(_iota((GLA_DV, GLA_PAIR), 1), C)
    st_lane_head = _div(_iota((GLA_DV, GLA_QK_W), 1), GLA_DK)

    for c in range(nb // C):
        rows = slice(c * C, (c + 1) * C)
        pair = slice((c // 2) * GLA_PAIR, (c // 2 + 1) * GLA_PAIR)
        b_c, k_c = b[rows], k[rows]
        kst = []
        for sidx in range(n_sub):
            r_row = r[c * C + sidx * S:c * C + sidx * S + 1, :]
            kk = k_c * jnp.exp(jnp.minimum(r_row - b_c, EXP_CLAMP))
            kst.append(jnp.where(srow < (sidx + 1) * S, kk, 0.0))
        kst = jnp.concatenate(kst, axis=0).astype(BF16)
        qst = jnp.where(head_sel, jnp.concatenate([q_intra[rows]] * GLA_HEADS, axis=0), 0.0).astype(BF16)
        att = jnp.where(att_keep, _dot_nt(qst, kst), 0.0).astype(BF16)
        qin = jnp.where(head_sel, jnp.concatenate([q_inter[rows]] * GLA_HEADS, axis=0), 0.0).astype(BF16)
        o_inter = _dot_nt(qin, st_ref[bb].astype(BF16))
        upd = jnp.zeros((GLA_DV, GLA_QK_W), F32)
        for h in range(GLA_HEADS):
            vcol = slice(2 * GLA_QK_W + h * GLA_DV, 2 * GLA_QK_W + (h + 1) * GLA_DV)
            gcol = slice(2 * GLA_QK_W + GLA_V_W + h * GLA_DV, 2 * GLA_QK_W + GLA_V_W + (h + 1) * GLA_DV)
            v_h = zg_ref[rows, vcol].astype(BF16)
            o = _dot(att[h * C:(h + 1) * C], jnp.concatenate([v_h] * n_sub, axis=0)) + o_inter[h * C:(h + 1) * C]
            gate = zg_ref[rows, gcol]
            a_ref[bb, rows, h * GLA_DV:(h + 1) * GLA_DV] = (
                _rms(o, norm_ref[...]) * (gate * _sigmoid(gate))).astype(BF16)
            v_t = zg_ref[pair, vcol].T
            u_t = _dot(jnp.where(pair_chunk == (c % 2), v_t, 0.0).astype(BF16), k_state[pair])
            upd = upd + jnp.where(st_lane_head == h, u_t, 0.0)
        st_ref[bb] = st_ref[bb] * decay[c * C:c * C + 1, :] + upd


def _gla_prompt_body(n_par, *refs):
    zg_refs, la_refs = refs[:n_par], refs[n_par:2 * n_par]
    norm_ref, cum_ref, a_ref, sout_ref, st_ref = refs[2 * n_par:]
    i = pl.program_id(1)

    @pl.when(i == 0)
    def _():
        st_ref[...] = jnp.zeros_like(st_ref)

    for bb in range(n_par):
        _gla_seq_block(bb, zg_refs[bb], la_refs[bb], norm_ref, cum_ref, a_ref, st_ref)

    @pl.when(i == pl.num_programs(1) - 1)
    def _():
        for bb in range(n_par):
            sout_ref[bb] = st_ref[bb].T.reshape(GLA_HEADS, GLA_DK, GLA_DV)


def _gla_prompt(zg, la, norm, batch, seq):
    n_par = GLA_PAR if batch % GLA_PAR == 0 else 1
    nblk = seq // GLA_BLOCK
    cum = jnp.asarray(_cum_matrices(GLA_BLOCK, GLA_CHUNK, GLA_SUB), BF16)
    rows = [lambda g, i, bb=bb: ((g * n_par + bb) * nblk + i, 0) for bb in range(n_par)]
    return pl.pallas_call(
        functools.partial(_gla_prompt_body, n_par),
        grid=(batch // n_par, nblk),
        in_specs=[
            *[pl.BlockSpec((GLA_BLOCK, ZG_W), r) for r in rows],
            *[pl.BlockSpec((GLA_BLOCK, GLA_QK_W), r) for r in rows],
            pl.BlockSpec((1, GLA_DV), lambda g, i: (0, 0)),
            pl.BlockSpec(cum.shape, lambda g, i: (0, 0)),
        ],
        out_specs=[
            pl.BlockSpec((n_par, GLA_BLOCK, GLA_V_W), lambda g, i: (g, i, 0)),
            pl.BlockSpec((n_par, GLA_HEADS, GLA_DK, GLA_DV), lambda g, i: (g, 0, 0, 0)),
        ],
        out_shape=(jax.ShapeDtypeStruct((batch, seq, GLA_V_W), BF16),
                   jax.ShapeDtypeStruct((batch, GLA_HEADS, GLA_DK, GLA_DV), F32)),
        scratch_shapes=[pltpu.VMEM((n_par, GLA_DV, GLA_QK_W), F32)],
        compiler_params=pltpu.CompilerParams(dimension_semantics=("arbitrary", "arbitrary"),
                                             vmem_limit_bytes=VMEM_LIMIT),
        name="gla_prompt",
    )(*([zg] * n_par), *([la] * n_par), norm, cum)


GLA_S_ROWS = 128


def _gla_sample_body(n_new, zg_ref, la_ref, s_ref, norm_ref, cum_ref, a_ref, sout_ref,
                     qin_ref, khat_ref, dec_ref, oint_ref, vt_ref):
    rows = GLA_S_ROWS
    cums = _dot01(cum_ref[...], la_ref[...])
    b = cums[0:rows]
    bl = cums[rows:2 * rows]
    q = zg_ref[:, 0:GLA_QK_W] * GLA_DK ** -0.5
    k = zg_ref[:, GLA_QK_W:2 * GLA_QK_W]
    q_dec = q * jnp.exp(b)
    k_intra = (k * jnp.exp(jnp.minimum(-b, EXP_CLAMP))).astype(BF16)
    qin_ref[...] = q_dec
    khat_ref[...] = (k * jnp.exp(bl - b)).astype(BF16)
    dec_ref[...] = jnp.exp(bl)

    hr = GLA_HEADS * rows
    head_sel = _div(_iota((hr, GLA_QK_W), 1), GLA_DK) == _div(_iota((hr, GLA_QK_W), 0), rows)
    qst = jnp.where(head_sel, jnp.concatenate([q_dec] * GLA_HEADS, axis=0), 0.0).astype(BF16)
    tt = _mod(_iota((hr, rows), 0), rows)
    ss = _iota((hr, rows), 1)
    keep = (_div(tt, n_new) == _div(ss, n_new)) & (ss <= tt)
    att = jnp.where(keep, _dot_nt(qst, k_intra), 0.0).astype(BF16)
    for h in range(GLA_HEADS):
        vcol = slice(2 * GLA_QK_W + h * GLA_DV, 2 * GLA_QK_W + (h + 1) * GLA_DV)
        v_h = zg_ref[:, vcol]
        oint_ref[:, h * GLA_DV:(h + 1) * GLA_DV] = _dot(att[h * rows:(h + 1) * rows], v_h.astype(BF16))
        vt_ref[h] = v_h.T

    per8 = SUBLANES // n_new
    sub8 = _div(_iota((SUBLANES, GLA_QK_W), 0), n_new)
    lane8 = _div(_iota((SUBLANES, GLA_QK_W), 1), GLA_DK)
    out8 = _div(_iota((SUBLANES, GLA_DV), 0), n_new)
    vt_seq = _div(_iota((GLA_DV, rows), 1), n_new)
    st_lane_head = _div(_iota((GLA_DV, GLA_QK_W), 1), GLA_DK)

    def group_step(m, carry):
        r8 = pl.ds(pl.multiple_of(m * SUBLANES, SUBLANES), SUBLANES)
        q8 = qin_ref[r8, :]
        d8 = dec_ref[r8, :]
        res = []
        for e in range(per8):
            n = m * per8 + e
            s_n = s_ref[0, n].reshape(GLA_HEADS * GLA_DK, GLA_DV)
            lhs = jnp.concatenate([jnp.where((sub8 == e) & (lane8 == h), q8, 0.0) for h in range(GLA_HEADS)],
                                  axis=0)
            res.append(_dot(lhs.astype(BF16), s_n.astype(BF16)))
            upd = jnp.zeros((GLA_DV, GLA_QK_W), F32)
            for h in range(GLA_HEADS):
                u_t = _dot(jnp.where(vt_seq == n, vt_ref[h], 0.0).astype(BF16), khat_ref[...])
                upd = upd + jnp.where(st_lane_head == h, u_t, 0.0)
            s_new_t = s_n.T * d8[e * n_new:e * n_new + 1, :] + upd
            sout_ref[0, n] = s_new_t.T.reshape(GLA_HEADS, GLA_DK, GLA_DV)
        for h in range(GLA_HEADS):
            o8 = res[0][h * SUBLANES:(h + 1) * SUBLANES]
            for e in range(1, per8):
                o8 = jnp.where(out8 == e, res[e][h * SUBLANES:(h + 1) * SUBLANES], o8)
            oint_ref[r8, h * GLA_DV:(h + 1) * GLA_DV] += o8
        return carry

    lax.fori_loop(0, rows // SUBLANES, group_step, 0)

    for h in range(GLA_HEADS):
        gcol = slice(2 * GLA_QK_W + GLA_V_W + h * GLA_DV, 2 * GLA_QK_W + GLA_V_W + (h + 1) * GLA_DV)
        gate = zg_ref[:, gcol]
        o = oint_ref[:, h * GLA_DV:(h + 1) * GLA_DV]
        a_ref[:, h * GLA_DV:(h + 1) * GLA_DV] = (_rms(o, norm_ref[...]) * (gate * _sigmoid(gate))).astype(BF16)


def _gla_sample(zg, la, state, norm, row0, n_seq, n_new):
    assert SUBLANES % n_new == 0 and GLA_S_ROWS % n_new == 0
    seqs = GLA_S_ROWS // n_new
    steps = n_seq // seqs
    blk0 = row0 // GLA_S_ROWS
    cum = jnp.asarray(_cum_matrices(GLA_S_ROWS, n_new), BF16)
    return pl.pallas_call(
        functools.partial(_gla_sample_body, n_new),
        grid=(steps,),
        in_specs=[
            pl.BlockSpec((GLA_S_ROWS, ZG_W), lambda i: (blk0 + i, 0)),
            pl.BlockSpec((GLA_S_ROWS, GLA_QK_W), lambda i: (blk0 + i, 0)),
            pl.BlockSpec((1, seqs, GLA_HEADS, GLA_DK, GLA_DV), lambda i: (0, i, 0, 0, 0)),
            pl.BlockSpec((1, GLA_DV), lambda i: (0, 0)),
            pl.BlockSpec(cum.shape, lambda i: (0, 0)),
        ],
        out_specs=[
            pl.BlockSpec((GLA_S_ROWS, GLA_V_W), lambda i: (i, 0)),
            pl.BlockSpec((1, seqs, GLA_HEADS, GLA_DK, GLA_DV), lambda i: (0, i, 0, 0, 0)),
        ],
        out_shape=(jax.ShapeDtypeStruct((n_seq * n_new, GLA_V_W), BF16),
                   jax.ShapeDtypeStruct((1, n_seq, GLA_HEADS, GLA_DK, GLA_DV), F32)),
        scratch_shapes=[
            pltpu.VMEM((GLA_S_ROWS, GLA_QK_W), F32),
            pltpu.VMEM((GLA_S_ROWS, GLA_QK_W), BF16),
            pltpu.VMEM((GLA_S_ROWS, GLA_QK_W), F32),
            pltpu.VMEM((GLA_S_ROWS, GLA_V_W), F32),
            pltpu.VMEM((GLA_HEADS, GLA_DV, GLA_S_ROWS), F32),
        ],
        compiler_params=pltpu.CompilerParams(dimension_semantics=("arbitrary",),
                                             vmem_limit_bytes=VMEM_LIMIT),
        name="gla_sample",
    )(zg, la, state, norm, cum)


SB_TILE = 256
SB_PAIR = LANES // SB_DH
SB_HPS = 4
SB_BLK = SB_HPS * SB_DH
SB_GROUP = 2
SB_MASKED = -1e30


def _sb_prompt_body(bias_ref, q_ref, k_ref, v_ref, norm_ref, u_ref, o_ref, acc_ref, carry_ref, bias_buf, zs_buf,
                    cs_buf, tot_buf):
    p = pl.program_id(1)
    i = pl.program_id(2)
    nq = pl.num_programs(2) - 1
    qi = jnp.minimum(i, nq - 1)
    qprev = jnp.maximum(i - 1, 0)
    T = SB_TILE
    lane_head = _div(_iota((T, SB_BLK), 1), SB_DH)
    q = q_ref[...]
    qh = [jnp.where(lane_head == e, q, jnp.zeros_like(q)) for e in range(SB_HPS)]
    u = u_ref[...]

    @pl.when(i == 0)
    def _():
        valid = _iota((T, T), 1) < _iota((T, T), 0)
        for e in range(SB_HPS):
            b = bias_ref[p * SB_HPS + e]
            bias_buf[e, 0] = jnp.where(valid, b, SB_MASKED)
            bias_buf[e, 1] = jnp.full((T, T), b, F32)
            bias_buf[e, 2] = jnp.full((T, T), SB_MASKED, F32)
        acc_ref[...] = jnp.zeros_like(acc_ref)
        carry_ref[...] = jnp.zeros_like(carry_ref)
        zs_buf[...] = jnp.zeros_like(zs_buf)
        cs_buf[...] = jnp.zeros_like(cs_buf)
        tot_buf[...] = jnp.zeros_like(tot_buf)

    def groups_of(qt):
        return _div(qt + SB_GROUP, SB_GROUP)

    def tile_of(qt, t):
        kj = qt - t
        kind = jnp.where(t == 0, 0, jnp.where(kj >= 0, 1, 2))
        return kind, pl.ds(pl.multiple_of(jnp.maximum(kj, 0) * T, T), T)

    def logits_stage(g):
        for j in range(SB_GROUP):
            kind, rows = tile_of(qi, g * SB_GROUP + j)
            kt = k_ref[rows, :]
            for e in range(SB_HPS):
                z = _dot_nt(qh[e], kt) + bias_buf[e, kind]
                sp = _softplus2(z)
                zs_buf[j, e] = z - sp
                cs = _dot(sp.astype(BF16), u)
                cs_buf[j, e] = cs
                tot_buf[j, e] = cs[:, 0:1] + sp[:, 0:1]

    def values_stage(qt, g):
        for j in range(SB_GROUP):
            _, rows = tile_of(qt, g * SB_GROUP + j)
            vt = v_ref[rows, :]
            for e in range(SB_HPS):
                w = jnp.exp2(zs_buf[j, e] - cs_buf[j, e])
                own = slice((e // SB_PAIR) * LANES, (e // SB_PAIR + 1) * LANES)
                acc_ref[e] += jnp.exp2(-carry_ref[e]) * _dot(w.astype(BF16), vt)[:, own]
                carry_ref[e] += tot_buf[j, e]

    values_stage(qprev, groups_of(qprev) - 1)

    half_head = _div(_iota((T, LANES), 1), SB_DH)
    for blk in range(SB_HPS // SB_PAIR):
        o = acc_ref[blk * SB_PAIR]
        for e in range(1, SB_PAIR):
            o = jnp.where(half_head == e, acc_ref[blk * SB_PAIR + e], o)
        sq = o * o
        ms = jnp.zeros_like(o)
        for e in range(SB_PAIR):
            ms_e = jnp.sum(jnp.where(half_head == e, sq, 0.0), axis=1, keepdims=True) * (1.0 / SB_DH)
            ms = jnp.where(half_head == e, ms_e, ms)
        lanes = slice(blk * LANES, (blk + 1) * LANES)
        o_ref[:, lanes] = (o * lax.rsqrt(ms + EPS) * norm_ref[:, lanes]).astype(BF16)

    acc_ref[...] = jnp.zeros_like(acc_ref)
    carry_ref[...] = jnp.zeros_like(carry_ref)
    logits_stage(0)

    def step(g, c):
        values_stage(qi, g - 1)
        logits_stage(g)
        return c

    lax.fori_loop(1, jnp.where(i < nq, groups_of(qi), 1), step, 0)


def _sb_prompt(bias, sq_b, sk_b, sv_b, norm_blk, batch, seq):
    nq = seq // SB_TILE
    nblk = SB_HEADS // SB_HPS
    j = np.arange(SB_TILE)[:, None]
    s = np.arange(SB_TILE)[None, :]
    u = jnp.asarray((j > s).astype(np.float32), BF16)
    return pl.pallas_call(
        _sb_prompt_body,
        grid=(batch, nblk, nq + 1),
        in_specs=[
            pl.BlockSpec(memory_space=pltpu.SMEM),
            pl.BlockSpec((SB_TILE, SB_BLK), lambda b, p, i: (b * nq + jnp.minimum(i, nq - 1), p)),
            pl.BlockSpec((seq, SB_BLK), lambda b, p, i: (b, p)),
            pl.BlockSpec((seq, SB_BLK), lambda b, p, i: (b, p)),
            pl.BlockSpec((1, SB_BLK), lambda b, p, i: (0, 0)),
            pl.BlockSpec((SB_TILE, SB_TILE), lambda b, p, i: (0, 0)),
        ],
        out_specs=pl.BlockSpec((SB_TILE, SB_BLK), lambda b, p, i: (b * nq + jnp.maximum(i - 1, 0), p)),
        out_shape=jax.ShapeDtypeStruct((batch * seq, SB_W), BF16),
        scratch_shapes=[
            pltpu.VMEM((SB_HPS, SB_TILE, LANES), F32),
            pltpu.VMEM((SB_HPS, SB_TILE, 1), F32),
            pltpu.VMEM((SB_HPS, 3, SB_TILE, SB_TILE), F32),
            pltpu.VMEM((SB_GROUP, SB_HPS, SB_TILE, SB_TILE), F32),
            pltpu.VMEM((SB_GROUP, SB_HPS, SB_TILE, SB_TILE), F32),
            pltpu.VMEM((SB_GROUP, SB_HPS, SB_TILE, 1), F32),
        ],
        compiler_params=pltpu.CompilerParams(dimension_semantics=("arbitrary", "arbitrary", "arbitrary"),
                                             vmem_limit_bytes=VMEM_LIMIT),
        name="sb_prompt",
    )(bias, sq_b, sk_b, sv_b, norm_blk, u)


SBS_PAR = 1


def _sb_scan_matrix():
    s_in = np.arange(LANES)[:, None]
    s_out = np.arange(LANES)[None, :]
    return np.concatenate([s_in > s_out, np.ones((LANES, LANES), bool)], axis=1).astype(np.float32)


def _sb_sample_seq(n_new, n_pages, bias, up, norm, q8, kn8, vn8, k_refs, v_refs):
    qrows = n_new * SB_HEADS
    own_head = _mod(_iota((qrows, SB_W), 0), SB_HEADS) == _div(_iota((qrows, SB_W), 1), SB_DH)
    q_rows = jnp.concatenate([jnp.broadcast_to(q8[t:t + 1, :], (SB_HEADS, SB_W)) for t in range(n_new)], axis=0)
    qbd = jnp.where(own_head, q_rows, 0.0).astype(BF16)

    zs = [_dot(qbd, k_refs[g][...].reshape(SB_W, PAGE_SIZE).astype(BF16)) + bias for g in range(n_pages)]
    pad = jnp.zeros((LANES - SUBLANES, SB_W), F32)
    kn = jnp.concatenate([kn8, pad], axis=0).astype(BF16)
    vn = jnp.concatenate([vn8, pad], axis=0).astype(BF16)
    zs.append(_dot_nt(qbd, kn) + bias)
    new_ok = _iota((qrows, LANES), 1) < _div(_iota((qrows, LANES), 0), SB_HEADS)
    sps = [_softplus2(z) for z in zs]
    sps[n_pages] = jnp.where(new_ok, sps[n_pages], 0.0)
    r = _dot(jnp.concatenate(sps, axis=0).astype(BF16), up)
    run = jnp.zeros((qrows, LANES), F32)
    acc = jnp.zeros((qrows, SB_W), F32)
    for g in reversed(range(n_pages + 1)):
        blk = r[g * qrows:(g + 1) * qrows]
        w = jnp.exp2(zs[g] - sps[g] - blk[:, 0:LANES] - run)
        run = run + blk[:, LANES:2 * LANES]
        if g == n_pages:
            acc = acc + _dot(jnp.where(new_ok, w, 0.0).astype(BF16), vn)
        else:
            acc = acc + _dot_nt(w.astype(BF16), v_refs[g][...].reshape(SB_W, PAGE_SIZE).astype(BF16))
    acc = jnp.where(own_head, acc, 0.0)
    o = acc[:, 0:SB_DH]
    for h in range(1, SB_HEADS):
        o = o + acc[:, h * SB_DH:(h + 1) * SB_DH]
    return _rms(o, norm).astype(BF16)


def _sb_sample_body(n_new, n_pages, n_par, pt_ref, bias_ref, q_ref, kn_ref, vn_ref, *rest):
    del pt_ref
    k_refs, v_refs = rest[:n_par * n_pages], rest[n_par * n_pages:2 * n_par * n_pages]
    norm_ref, up_ref, o_ref = rest[2 * n_par * n_pages:]
    for s in range(n_par):
        pages = slice(s * n_pages, (s + 1) * n_pages)
        o_ref[s] = _sb_sample_seq(n_new, n_pages, bias_ref[...], up_ref[...], norm_ref[...], q_ref[s], kn_ref[s],
                                  vn_ref[s], k_refs[pages], v_refs[pages])


def _sb_sample(page_table, bias_rows, q_s, k_new, v_new, pool_kt, pool_vt, norm, layer, n_new):
    n_seq, n_pages = page_table.shape
    qrows = n_new * SB_HEADS
    assert PAGE_SIZE == LANES and n_new <= SUBLANES
    up = jnp.asarray(_sb_scan_matrix(), BF16)
    n_par = SBS_PAR if n_seq % SBS_PAR == 0 else 1
    seq3 = lambda b, pt: (b, 0, 0)
    const = lambda b, pt: (0, 0)
    page_specs = [pl.BlockSpec((None, None, SB_HEADS, SB_DH, PAGE_SIZE),
                               lambda b, pt, s=s, g=g: (layer, pt[(b * n_par + s) * n_pages + g], 0, 0, 0))
                  for s in range(n_par) for g in range(n_pages)]
    grid_spec = pltpu.PrefetchScalarGridSpec(
        num_scalar_prefetch=1,
        grid=(n_seq // n_par,),
        in_specs=[
            pl.BlockSpec(bias_rows.shape, const),
            pl.BlockSpec((n_par, SUBLANES, SB_W), seq3),
            pl.BlockSpec((n_par, SUBLANES, SB_W), seq3),
            pl.BlockSpec((n_par, SUBLANES, SB_W), seq3),
            *page_specs, *page_specs,
            pl.BlockSpec((1, SB_DH), const),
            pl.BlockSpec(up.shape, const),
        ],
        out_specs=pl.BlockSpec((n_par, qrows, SB_DH), seq3),
    )
    return pl.pallas_call(
        functools.partial(_sb_sample_body, n_new, n_pages, n_par),
        grid_spec=grid_spec,
        out_shape=jax.ShapeDtypeStruct((n_seq, qrows, SB_DH), BF16),
        compiler_params=pltpu.CompilerParams(dimension_semantics=("arbitrary",),
                                             vmem_limit_bytes=VMEM_LIMIT),
        name="sb_sample",
    )(page_table.reshape(-1), bias_rows, q_s, k_new, v_new, *([pool_kt] * (n_par * n_pages)),
      *([pool_vt] * (n_par * n_pages)), norm, up)


def kernel(x_prompt, x_sample, cache_sb_k, cache_sb_v, state_gla, page_table, p_prompt, p_sample, f1_pre, f1_in, f1_out, f1_post, mix_pre, w_in, gate_w, gate_b, gla_norm, sb_norm, sb_bias, w_out, mix_post, f2_pre, f2_in, f2_out, f2_post, ple_pre, ple_w_gate, ple_w_up, ple_post):
    batch, seq, _ = x_prompt.shape
    n_seq, n_new, _ = x_sample.shape
    depth = f1_in.shape[0]
    n_p, n_s = batch * seq, n_seq * n_new
    assert n_p % TOK_TILE == 0 and n_s % TOK_TILE == 0 and seq % SB_TILE == 0 and n_p % GLA_S_ROWS == 0

    assert seq % TOK_TILE == 0
    x_p, x_s = x_prompt.reshape(n_p, D_MODEL), x_sample.reshape(n_s, D_MODEL)
    vec = lambda a: a.reshape(1, -1)
    lr0 = ZG_W
    sb0 = ZG_W + GLA_RANK
    pk, pv, ps, sk_l, sv_l, ss = [], [], [], [], [], []
    for l in range(depth):
        wcat = jnp.concatenate(
            [w_in[l][:, :lr0], w_in[l][:, sb0:], w_in[l][:, lr0:sb0],
             jnp.zeros((D_MODEL, LANES - GLA_RANK), F32)], axis=1).astype(BF16)
        gw_pad = jnp.concatenate([gate_w[l], jnp.zeros((LANES - GLA_RANK, GLA_QK_W), F32)], axis=0)
        x1, zg, la, sq_b, sk_b, sv_b, sk_t, sv_t, sk_s, sv_s = _ffn_in(
            x_p, x_s, batch, vec(f1_pre[l]), f1_in[l].astype(BF16), f1_out[l].astype(BF16), vec(f1_post[l]),
            vec(mix_pre[l]), wcat, gw_pad, vec(gate_b[l]))

        a_p, s_gla_p = _gla_prompt(zg, la, vec(gla_norm[l]), batch, seq)
        a_p = a_p.reshape(n_p, GLA_V_W)
        bias2 = sb_bias[l] * LOG2E
        o_sb_p = _sb_prompt(bias2, sq_b, sk_b, sv_b, vec(jnp.tile(sb_norm[l], SB_HPS)), batch, seq)

        a_s, s_gla_s = _gla_sample(zg, la, state_gla[l:l + 1], vec(gla_norm[l]), n_p, n_seq, n_new)
        rows8 = lambda a: jnp.pad(a.astype(F32).reshape(n_seq, n_new, SB_W),
                                  ((0, 0), (0, SUBLANES - n_new), (0, 0)))
        bias_rows = jnp.broadcast_to(jnp.tile(bias2, n_new)[:, None], (n_new * SB_HEADS, LANES))
        to_hdk = lambda c: jnp.transpose(c, (0, 1, 3, 4, 2))
        o_sb_s = _sb_sample(page_table, bias_rows, rows8(sq_b[n_p:]), rows8(sk_s), rows8(sv_s),
                            to_hdk(cache_sb_k), to_hdk(cache_sb_v), vec(sb_norm[l]), l, n_new)

        x_p, x_s = _merge_out(
            x1, a_p, a_s, o_sb_p, o_sb_s.reshape(n_s, SB_W), p_prompt[l].reshape(n_p, PLE_DIM),
            p_sample[l].reshape(n_s, PLE_DIM), w_out[l].astype(BF16), vec(mix_post[l]), vec(f2_pre[l]),
            f2_in[l].astype(BF16), f2_out[l].astype(BF16), vec(f2_post[l]), vec(ple_pre[l]),
            ple_w_gate[l].astype(BF16), ple_w_up[l].astype(BF16), vec(ple_post[l]))

        pk.append(jnp.transpose(sk_t, (0, 3, 1, 2)))
        pv.append(jnp.transpose(sv_t, (0, 3, 1, 2)))
        ps.append(s_gla_p)
        sk_l.append(sk_s.reshape(n_seq, n_new, SB_HEADS, SB_DH))
        sv_l.append(sv_s.reshape(n_seq, n_new, SB_HEADS, SB_DH))
        ss.append(s_gla_s[0])
    return (x_p.reshape(batch, seq, D_MODEL), x_s.reshape(n_seq, n_new, D_MODEL),
            jnp.stack(pk), jnp.stack(pv), jnp.stack(ps), jnp.stack(sk_l), jnp.stack(sv_l), jnp.stack(ss))
```

```python
import functools

import numpy as np
import jax
import jax.numpy as jnp
from jax import lax
from jax.experimental import pallas as pl
from jax.experimental.pallas import tpu as pltpu

F32 = jnp.float32
BF16 = jnp.bfloat16

D_MODEL = 1024
D_FF = 2816
PLE_DIM = 256
GLA_HEADS = 4
GLA_DK = 64
GLA_DV = 128
GLA_RANK = 16
GLA_GATE_TEMP = 16.0
GLA_CHUNK = 64
GLA_SUB = 16
SB_HEADS = 8
SB_DH = 64
PAGE_SIZE = 128
EPS = 1e-6
GLA_QK_W = GLA_HEADS * GLA_DK
GLA_V_W = GLA_HEADS * GLA_DV
SB_W = SB_HEADS * SB_DH
ZG_W = 2 * GLA_QK_W + 2 * GLA_V_W
LANES = 128
SUBLANES = 8
LOG2E = 1.4426950408889634

TOK_TILE = 512
FF_TILE = 256
VMEM_LIMIT = 62 * 1024 * 1024


def _log2(n):
    k = int(n).bit_length() - 1
    assert 1 << k == n, n
    return k


def _div(x, n):
    return lax.shift_right_logical(x, _log2(n))


def _mod(x, n):
    assert n & (n - 1) == 0
    return x & (n - 1)


def _iota(shape, dim):
    return lax.broadcasted_iota(jnp.int32, shape, dim)


def _dot(a, b):
    return jnp.dot(a, b, preferred_element_type=F32)


def _dot_nt(a, b):
    return lax.dot_general(a, b, (((1,), (1,)), ((), ())), preferred_element_type=F32)


def _dot01(t_bf16, x):
    hi = x.astype(BF16)
    lo = (x - hi.astype(F32)).astype(BF16)
    return _dot(t_bf16, hi) + _dot(t_bf16, lo)


def _rms(x, w):
    ms = jnp.mean(x * x, axis=-1, keepdims=True)
    return x * lax.rsqrt(ms + EPS) * w


def _sigmoid(x):
    return 1.0 / (1.0 + jnp.exp(-x))


def _softplus2(x):
    return jnp.maximum(x, jnp.log(1.0 + jnp.exp2(jnp.minimum(x, 126.0))) * LOG2E)


def _swiglu_ffn(h, win_ref, wout_ref, acc_ref):
    for j in range(D_FF // FF_TILE):
        g = _dot(h, win_ref[:, j * FF_TILE:(j + 1) * FF_TILE])
        u = _dot(h, win_ref[:, D_FF + j * FF_TILE:D_FF + (j + 1) * FF_TILE])
        act = (g * _sigmoid(g) * u).astype(BF16)
        part = _dot(act, wout_ref[j * FF_TILE:(j + 1) * FF_TILE, :])
        if j == 0:
            acc_ref[...] = part
        else:
            acc_ref[...] += part


def _resident(shape):
    return pl.BlockSpec(shape, lambda i: (0,) * len(shape), pipeline_mode=pl.Buffered(1))


def _ffn_in_body(n_ptiles, xp_ref, xs_ref, pre_ref, win_ref, wout_ref, post_ref, mixpre_ref, wcat_ref, gw_ref,
                 gb_ref, x1_ref, zg_ref, la_ref, sq_ref, skb_ref, svb_ref, skt_ref, svt_ref, sks_ref, svs_ref,
                 acc_ref):
    is_prompt = pl.program_id(0) < n_ptiles
    x = jnp.where(is_prompt, xp_ref[...], xs_ref[...])
    _swiglu_ffn(_rms(x, pre_ref[...]).astype(BF16), win_ref, wout_ref, acc_ref)
    x1 = x + 0.5 * _rms(acc_ref[...], post_ref[...])
    x1_ref[...] = x1
    h2 = _rms(x1, mixpre_ref[...]).astype(BF16)
    sb = _dot(h2, wcat_ref[:, ZG_W:ZG_W + 3 * SB_W])
    sq_ref[...] = (sb[:, 0:SB_W] * (SB_DH ** -0.5 * LOG2E)).astype(BF16)
    sk = sb[:, SB_W:2 * SB_W]
    sv = sb[:, 2 * SB_W:3 * SB_W]
    skb_ref[...] = sk.astype(BF16)
    svb_ref[...] = sv.astype(BF16)
    sks_ref[...] = sk
    svs_ref[...] = sv
    sk_t = sk.T.reshape(SB_HEADS, SB_DH, TOK_TILE)
    sv_t = sv.T.reshape(SB_HEADS, SB_DH, TOK_TILE)
    lr = _dot(h2, wcat_ref[:, ZG_W + 3 * SB_W:])
    logit = jnp.dot(lr, gw_ref[...], precision=lax.Precision.HIGHEST,
                    preferred_element_type=F32) + gb_ref[...]
    la_ref[...] = (jnp.minimum(logit, 0.0) - jnp.log1p(jnp.exp(-jnp.abs(logit)))) * (1.0 / GLA_GATE_TEMP)
    zg_ref[...] = _dot(h2, wcat_ref[:, 0:ZG_W])

    @pl.when(is_prompt)
    def _():
        skt_ref[0] = sk_t
        svt_ref[0] = sv_t


def _ffn_in(x_p, x_s, batch, pre, w_in_b, w_out_b, post, mixpre, wcat_b, gw_pad, gb):
    n_p, n_s = x_p.shape[0], x_s.shape[0]
    n = n_p + n_s
    seq = n_p // batch
    n_pt, per_seq = n_p // TOK_TILE, seq // TOK_TILE
    row = lambda i: (i, 0)
    p_row = lambda i: (jnp.minimum(i, n_pt - 1), 0)
    s_row = lambda i: (jnp.maximum(i - n_pt, 0), 0)
    p_kv = lambda i: (jnp.minimum(i, n_pt - 1) // per_seq, 0, 0, jnp.minimum(i, n_pt - 1) % per_seq)
    vecd = _resident((1, D_MODEL))
    out_shapes = (
        jax.ShapeDtypeStruct((n, D_MODEL), F32),
        jax.ShapeDtypeStruct((n, ZG_W), F32),
        jax.ShapeDtypeStruct((n, GLA_QK_W), F32),
        jax.ShapeDtypeStruct((n, SB_W), BF16),
        jax.ShapeDtypeStruct((n, SB_W), BF16),
        jax.ShapeDtypeStruct((n, SB_W), BF16),
        jax.ShapeDtypeStruct((batch, SB_HEADS, SB_DH, seq), F32),
        jax.ShapeDtypeStruct((batch, SB_HEADS, SB_DH, seq), F32),
        jax.ShapeDtypeStruct((n_s, SB_W), F32),
        jax.ShapeDtypeStruct((n_s, SB_W), F32),
    )
    return pl.pallas_call(
        functools.partial(_ffn_in_body, n_pt),
        grid=(n // TOK_TILE,),
        in_specs=[
            pl.BlockSpec((TOK_TILE, D_MODEL), p_row),
            pl.BlockSpec((TOK_TILE, D_MODEL), s_row),
            vecd,
            _resident(w_in_b.shape),
            _resident(w_out_b.shape),
            vecd, vecd,
            _resident(wcat_b.shape),
            _resident(gw_pad.shape),
            _resident((1, GLA_QK_W)),
        ],
        out_specs=[
            pl.BlockSpec((TOK_TILE, D_MODEL), row),
            pl.BlockSpec((TOK_TILE, ZG_W), row),
            pl.BlockSpec((TOK_TILE, GLA_QK_W), row),
            pl.BlockSpec((TOK_TILE, SB_W), row),
            pl.BlockSpec((TOK_TILE, SB_W), row),
            pl.BlockSpec((TOK_TILE, SB_W), row),
            pl.BlockSpec((1, SB_HEADS, SB_DH, TOK_TILE), p_kv),
            pl.BlockSpec((1, SB_HEADS, SB_DH, TOK_TILE), p_kv),
            pl.BlockSpec((TOK_TILE, SB_W), s_row),
            pl.BlockSpec((TOK_TILE, SB_W), s_row),
        ],
        out_shape=out_shapes,
        scratch_shapes=[pltpu.VMEM((TOK_TILE, D_MODEL), F32)],
        compiler_params=pltpu.CompilerParams(dimension_semantics=("arbitrary",),
                                             vmem_limit_bytes=VMEM_LIMIT),
        name="ffn_in",
    )(x_p, x_s, pre, w_in_b, w_out_b, post, mixpre, wcat_b, gw_pad, gb)


def _merge_out_body(n_ptiles, x1_ref, ap_ref, as_ref, sp_ref, ss_ref, pp_ref, ps_ref, wout_ref, mixpost_ref,
                    pre_ref, win_ref, woutf_ref, post_ref, plepre_ref, pleg_ref, pleu_ref, plepost_ref,
                    yp_ref, ys_ref, acc_ref):
    is_prompt = pl.program_id(0) < n_ptiles
    a = jnp.where(is_prompt, ap_ref[...], as_ref[...])
    s = jnp.where(is_prompt, sp_ref[...], ss_ref[...])
    p = jnp.where(is_prompt, pp_ref[...], ps_ref[...])
    m = _dot(a, wout_ref[0:GLA_V_W, :]) + _dot(s, wout_ref[GLA_V_W:, :])
    x2 = x1_ref[...] + _rms(m, mixpost_ref[...])
    _swiglu_ffn(_rms(x2, pre_ref[...]).astype(BF16), win_ref, woutf_ref, acc_ref)
    x3 = x2 + 0.5 * _rms(acc_ref[...], post_ref[...])
    hg = _rms(x3, plepre_ref[...]).astype(BF16)
    gate = _sigmoid(_dot(hg, pleg_ref[...]))
    e = _dot(p.astype(BF16), pleu_ref[...])
    y = x3 + _rms(gate * e, plepost_ref[...])

    @pl.when(is_prompt)
    def _():
        yp_ref[...] = y

    @pl.when(jnp.logical_not(is_prompt))
    def _():
        ys_ref[...] = y


def _merge_out(x1, a_p, a_s, s_p, s_s, p_p, p_s, wout_b, mixpost, pre, w_in_b, w_out_b, post, plepre, pleg_b,
               pleu_b, plepost):
    n_p, n_s = a_p.shape[0], a_s.shape[0]
    n_pt = n_p // TOK_TILE
    row = lambda i: (i, 0)
    p_row = lambda i: (jnp.minimum(i, n_pt - 1), 0)
    s_row = lambda i: (jnp.maximum(i - n_pt, 0), 0)
    pair = lambda width: [pl.BlockSpec((TOK_TILE, width), p_row), pl.BlockSpec((TOK_TILE, width), s_row)]
    vecd = _resident((1, D_MODEL))
    return pl.pallas_call(
        functools.partial(_merge_out_body, n_pt),
        grid=((n_p + n_s) // TOK_TILE,),
        in_specs=[
            pl.BlockSpec((TOK_TILE, D_MODEL), row),
            *pair(GLA_V_W), *pair(SB_W), *pair(PLE_DIM),
            _resident(wout_b.shape),
            vecd, vecd,
            _resident(w_in_b.shape),
            _resident(w_out_b.shape),
            vecd, vecd,
            _resident(pleg_b.shape),
            _resident(pleu_b.shape),
            vecd,
        ],
        out_specs=pair(D_MODEL),
        out_shape=(jax.ShapeDtypeStruct((n_p, D_MODEL), F32), jax.ShapeDtypeStruct((n_s, D_MODEL), F32)),
        scratch_shapes=[pltpu.VMEM((TOK_TILE, D_MODEL), F32)],
        compiler_params=pltpu.CompilerParams(dimension_semantics=("arbitrary",),
                                             vmem_limit_bytes=VMEM_LIMIT),
        name="merge_out",
    )(x1, a_p, a_s, s_p, s_s, p_p, p_s, wout_b, mixpost, pre, w_in_b, w_out_b, post, plepre, pleg_b, pleu_b,
      plepost)


GLA_BLOCK = 256
GLA_PAIR = 128
GLA_PAR = 4


def _halving_levels(block):
    return [block >> i for i in range(_log2(block))]


def _cum_matrices(block, chunk, sub=None, levels=()):
    t = np.arange(block)[:, None]
    s = np.arange(block)[None, :]
    same = (t // chunk) == (s // chunk)
    mats = [same & (s <= t)]
    if sub is not None:
        mats.append(same & ((s % chunk) < sub * ((t % chunk) // sub)))
    mats.append(same)
    for lvl in levels:
        mats.append(same & ((s % chunk) <= lvl * ((t % chunk) // lvl) + lvl // 2 - 1))
    return np.concatenate(mats, axis=0).astype(np.float32)


def _gla_block_scores(q, k, b, mids, levels, head_sel):
    rows = q.shape[0]
    hr = GLA_HEADS * rows
    t = _mod(_iota((hr, rows), 0), rows)
    s = _iota((hr, rows), 1)
    t_row = _iota((rows, GLA_QK_W), 0)
    stack = lambda a: jnp.where(head_sel, jnp.concatenate([a] * GLA_HEADS, axis=0), 0.0).astype(BF16)
    att = jnp.where(t == s, _dot_nt(stack(q), k.astype(BF16)), 0.0)
    for lvl, mid in zip(levels, mids):
        second = _mod(t_row, lvl) >= lvl // 2
        q_l = jnp.where(second, q * jnp.exp(jnp.minimum(b - mid, 0.0)), 0.0)
        k_l = jnp.where(second, 0.0, k * jnp.exp(jnp.minimum(mid - b, 0.0)))
        att = att + jnp.where(_div(t, lvl) == _div(s, lvl), _dot_nt(stack(q_l), k_l.astype(BF16)), 0.0)
    return att


def _gla_seq_block(bb, zg_ref, la_ref, norm_ref, cum_ref, a_ref, st_ref):
    nb = GLA_BLOCK
    levels = _halving_levels(GLA_SUB)
    cums = _dot01(cum_ref[...], la_ref[...])
    b = cums[0:nb]
    r = cums[nb:2 * nb]
    bl = cums[2 * nb:3 * nb]
    mids = [cums[(3 + n) * nb:(4 + n) * nb] for n in range(len(levels))]
    q = zg_ref[:, 0:GLA_QK_W] * GLA_DK ** -0.5
    k = zg_ref[:, GLA_QK_W:2 * GLA_QK_W]
    q_intra = q * jnp.exp(b - r)
    q_inter = q * jnp.exp(b)
    k_state = (k * jnp.exp(bl - b)).astype(BF16)
    decay = jnp.exp(bl)

    C, S = GLA_CHUNK, GLA_SUB
    n_sub = C // S
    hc = GLA_HEADS * C
    head_sel = _div(_iota((hc, GLA_QK_W), 1), GLA_DK) == _div(_iota((hc, GLA_QK_W), 0), C)
    tt = _mod(_iota((hc, n_sub * C), 0), C)
    cc = _iota((hc, n_sub * C), 1)
    att_keep = (_div(cc, C) == _div(tt, S)) & (_mod(cc, C) < S * _div(tt, S))
    srow = _iota((C, GLA_QK_W), 0)
    pair_chunk = _div(_iota((GLA_DV, GLA_PAIR), 1), C)
    st_lane_head = _div(_iota((GLA_DV, GLA_QK_W), 1), GLA_DK)

    for c in range(nb // C):
        rows = slice(c * C, (c + 1) * C)
        pair = slice((c // 2) * GLA_PAIR, (c // 2 + 1) * GLA_PAIR)
        b_c, k_c = b[rows], k[rows]
        kst = []
        for sidx in range(n_sub):
            r_row = r[c * C + sidx * S:c * C + sidx * S + 1, :]
            kk = k_c * jnp.exp(jnp.minimum(r_row - b_c, 0.0))
            kst.append(jnp.where(srow < sidx * S, kk, 0.0))
        kst = jnp.concatenate(kst, axis=0).astype(BF16)
        qst = jnp.where(head_sel, jnp.concatenate([q_intra[rows]] * GLA_HEADS, axis=0), 0.0).astype(BF16)
        att = jnp.where(att_keep, _dot_nt(qst, kst), 0.0).astype(BF16)
        att_in = _gla_block_scores(q[rows], k_c, b_c, [m[rows] for m in mids], levels, head_sel).astype(BF16)
        qin = jnp.where(head_sel, jnp.concatenate([q_inter[rows]] * GLA_HEADS, axis=0), 0.0).astype(BF16)
        o_inter = _dot_nt(qin, st_ref[bb].astype(BF16))
        upd = jnp.zeros((GLA_DV, GLA_QK_W), F32)
        for h in range(GLA_HEADS):
            vcol = slice(2 * GLA_QK_W + h * GLA_DV, 2 * GLA_QK_W + (h + 1) * GLA_DV)
            gcol = slice(2 * GLA_QK_W + GLA_V_W + h * GLA_DV, 2 * GLA_QK_W + GLA_V_W + (h + 1) * GLA_DV)
            v_h = zg_ref[rows, vcol].astype(BF16)
            o = (_dot(att[h * C:(h + 1) * C], jnp.concatenate([v_h] * n_sub, axis=0))
                 + _dot(att_in[h * C:(h + 1) * C], v_h) + o_inter[h * C:(h + 1) * C])
            gate = zg_ref[rows, gcol]
            a_ref[bb, rows, h * GLA_DV:(h + 1) * GLA_DV] = (
                _rms(o, norm_ref[...]) * (gate * _sigmoid(gate))).astype(BF16)
            v_t = zg_ref[pair, vcol].T
            u_t = _dot(jnp.where(pair_chunk == (c % 2), v_t, 0.0).astype(BF16), k_state[pair])
            upd = upd + jnp.where(st_lane_head == h, u_t, 0.0)
        st_ref[bb] = st_ref[bb] * decay[c * C:c * C + 1, :] + upd


def _gla_prompt_body(n_par, *refs):
    zg_refs, la_refs = refs[:n_par], refs[n_par:2 * n_par]
    norm_ref, cum_ref, a_ref, sout_ref, st_ref = refs[2 * n_par:]
    i = pl.program_id(1)

    @pl.when(i == 0)
    def _():
        st_ref[...] = jnp.zeros_like(st_ref)

    for bb in range(n_par):
        _gla_seq_block(bb, zg_refs[bb], la_refs[bb], norm_ref, cum_ref, a_ref, st_ref)

    @pl.when(i == pl.num_programs(1) - 1)
    def _():
        for bb in range(n_par):
            sout_ref[bb] = st_ref[bb].T.reshape(GLA_HEADS, GLA_DK, GLA_DV)


def _gla_prompt(zg, la, norm, batch, seq):
    n_par = GLA_PAR if batch % GLA_PAR == 0 else 1
    nblk = seq // GLA_BLOCK
    cum = jnp.asarray(_cum_matrices(GLA_BLOCK, GLA_CHUNK, GLA_SUB, _halving_levels(GLA_SUB)), BF16)
    rows = [lambda g, i, bb=bb: ((g * n_par + bb) * nblk + i, 0) for bb in range(n_par)]
    return pl.pallas_call(
        functools.partial(_gla_prompt_body, n_par),
        grid=(batch // n_par, nblk),
        in_specs=[
            *[pl.BlockSpec((GLA_BLOCK, ZG_W), r) for r in rows],
            *[pl.BlockSpec((GLA_BLOCK, GLA_QK_W), r) for r in rows],
            pl.BlockSpec((1, GLA_DV), lambda g, i: (0, 0)),
            pl.BlockSpec(cum.shape, lambda g, i: (0, 0)),
        ],
        out_specs=[
            pl.BlockSpec((n_par, GLA_BLOCK, GLA_V_W), lambda g, i: (g, i, 0)),
            pl.BlockSpec((n_par, GLA_HEADS, GLA_DK, GLA_DV), lambda g, i: (g, 0, 0, 0)),
        ],
        out_shape=(jax.ShapeDtypeStruct((batch, seq, GLA_V_W), BF16),
                   jax.ShapeDtypeStruct((batch, GLA_HEADS, GLA_DK, GLA_DV), F32)),
        scratch_shapes=[pltpu.VMEM((n_par, GLA_DV, GLA_QK_W), F32)],
        compiler_params=pltpu.CompilerParams(dimension_semantics=("arbitrary", "arbitrary"),
                                             vmem_limit_bytes=VMEM_LIMIT),
        name="gla_prompt",
    )(*([zg] * n_par), *([la] * n_par), norm, cum)


GLA_S_ROWS = 128


def _gla_sample_body(n_new, zg_ref, la_ref, s_ref, norm_ref, cum_ref, a_ref, sout_ref,
                     qin_ref, khat_ref, dec_ref, oint_ref, vt_ref):
    rows = GLA_S_ROWS
    levels = _halving_levels(n_new)
    cums = _dot01(cum_ref[...], la_ref[...])
    b = cums[0:rows]
    bl = cums[rows:2 * rows]
    mids = [cums[(2 + n) * rows:(3 + n) * rows] for n in range(len(levels))]
    q = zg_ref[:, 0:GLA_QK_W] * GLA_DK ** -0.5
    k = zg_ref[:, GLA_QK_W:2 * GLA_QK_W]
    qin_ref[...] = q * jnp.exp(b)
    khat_ref[...] = (k * jnp.exp(bl - b)).astype(BF16)
    dec_ref[...] = jnp.exp(bl)

    hr = GLA_HEADS * rows
    head_sel = _div(_iota((hr, GLA_QK_W), 1), GLA_DK) == _div(_iota((hr, GLA_QK_W), 0), rows)
    att = _gla_block_scores(q, k, b, mids, levels, head_sel).astype(BF16)
    for h in range(GLA_HEADS):
        vcol = slice(2 * GLA_QK_W + h * GLA_DV, 2 * GLA_QK_W + (h + 1) * GLA_DV)
        v_h = zg_ref[:, vcol]
        oint_ref[:, h * GLA_DV:(h + 1) * GLA_DV] = _dot(att[h * rows:(h + 1) * rows], v_h.astype(BF16))
        vt_ref[h] = v_h.T

    per8 = SUBLANES // n_new
    sub8 = _div(_iota((SUBLANES, GLA_QK_W), 0), n_new)
    lane8 = _div(_iota((SUBLANES, GLA_QK_W), 1), GLA_DK)
    out8 = _div(_iota((SUBLANES, GLA_DV), 0), n_new)
    vt_seq = _div(_iota((GLA_DV, rows), 1), n_new)
    st_lane_head = _div(_iota((GLA_DV, GLA_QK_W), 1), GLA_DK)

    def group_step(m, carry):
        r8 = pl.ds(pl.multiple_of(m * SUBLANES, SUBLANES), SUBLANES)
        q8 = qin_ref[r8, :]
        d8 = dec_ref[r8, :]
        res = []
        for e in range(per8):
            n = m * per8 + e
            s_n = s_ref[0, n].reshape(GLA_HEADS * GLA_DK, GLA_DV)
            lhs = jnp.concatenate([jnp.where((sub8 == e) & (lane8 == h), q8, 0.0) for h in range(GLA_HEADS)],
                                  axis=0)
            res.append(_dot(lhs.astype(BF16), s_n.astype(BF16)))
            upd = jnp.zeros((GLA_DV, GLA_QK_W), F32)
            for h in range(GLA_HEADS):
                u_t = _dot(jnp.where(vt_seq == n, vt_ref[h], 0.0).astype(BF16), khat_ref[...])
                upd = upd + jnp.where(st_lane_head == h, u_t, 0.0)
            s_new_t = s_n.T * d8[e * n_new:e * n_new + 1, :] + upd
            sout_ref[0, n] = s_new_t.T.reshape(GLA_HEADS, GLA_DK, GLA_DV)
        for h in range(GLA_HEADS):
            o8 = res[0][h * SUBLANES:(h + 1) * SUBLANES]
            for e in range(1, per8):
                o8 = jnp.where(out8 == e, res[e][h * SUBLANES:(h + 1) * SUBLANES], o8)
            oint_ref[r8, h * GLA_DV:(h + 1) * GLA_DV] += o8
        return carry

    lax.fori_loop(0, rows // SUBLANES, group_step, 0)

    for h in range(GLA_HEADS):
        gcol = slice(2 * GLA_QK_W + GLA_V_W + h * GLA_DV, 2 * GLA_QK_W + GLA_V_W + (h + 1) * GLA_DV)
        gate = zg_ref[:, gcol]
        o = oint_ref[:, h * GLA_DV:(h + 1) * GLA_DV]
        a_ref[:, h * GLA_DV:(h + 1) * GLA_DV] = (_rms(o, norm_ref[...]) * (gate * _sigmoid(gate))).astype(BF16)


def _gla_sample(zg, la, state, norm, row0, n_seq, n_new):
    assert SUBLANES % n_new == 0 and GLA_S_ROWS % n_new == 0
    seqs = GLA_S_ROWS // n_new
    steps = n_seq // seqs
    blk0 = row0 // GLA_S_ROWS
    cum = jnp.asarray(_cum_matrices(GLA_S_ROWS, n_new, levels=_halving_levels(n_new)), BF16)
    return pl.pallas_call(
        functools.partial(_gla_sample_body, n_new),
        grid=(steps,),
        in_specs=[
            pl.BlockSpec((GLA_S_ROWS, ZG_W), lambda i: (blk0 + i, 0)),
            pl.BlockSpec((GLA_S_ROWS, GLA_QK_W), lambda i: (blk0 + i, 0)),
            pl.BlockSpec((1, seqs, GLA_HEADS, GLA_DK, GLA_DV), lambda i: (0, i, 0, 0, 0)),
            pl.BlockSpec((1, GLA_DV), lambda i: (0, 0)),
            pl.BlockSpec(cum.shape, lambda i: (0, 0)),
        ],
        out_specs=[
            pl.BlockSpec((GLA_S_ROWS, GLA_V_W), lambda i: (i, 0)),
            pl.BlockSpec((1, seqs, GLA_HEADS, GLA_DK, GLA_DV), lambda i: (0, i, 0, 0, 0)),
        ],
        out_shape=(jax.ShapeDtypeStruct((n_seq * n_new, GLA_V_W), BF16),
                   jax.ShapeDtypeStruct((1, n_seq, GLA_HEADS, GLA_DK, GLA_DV), F32)),
        scratch_shapes=[
            pltpu.VMEM((GLA_S_ROWS, GLA_QK_W), F32),
            pltpu.VMEM((GLA_S_ROWS, GLA_QK_W), BF16),
            pltpu.VMEM((GLA_S_ROWS, GLA_QK_W), F32),
            pltpu.VMEM((GLA_S_ROWS, GLA_V_W), F32),
            pltpu.VMEM((GLA_HEADS, GLA_DV, GLA_S_ROWS), F32),
        ],
        compiler_params=pltpu.CompilerParams(dimension_semantics=("arbitrary",),
                                             vmem_limit_bytes=VMEM_LIMIT),
        name="gla_sample",
    )(zg, la, state, norm, cum)


SB_TILE = 256
SB_PAIR = LANES // SB_DH
SB_HPS = 4
SB_BLK = SB_HPS * SB_DH
SB_GROUP = 2
SB_MASKED = -1e30


def _sb_prompt_body(bias_ref, q_ref, k_ref, v_ref, norm_ref, u_ref, o_ref, acc_ref, carry_ref, bias_buf, zs_buf,
                    cs_buf, tot_buf):
    p = pl.program_id(1)
    i = pl.program_id(2)
    nq = pl.num_programs(2) - 1
    qi = jnp.minimum(i, nq - 1)
    qprev = jnp.maximum(i - 1, 0)
    T = SB_TILE
    lane_head = _div(_iota((T, SB_BLK), 1), SB_DH)
    q = q_ref[...]
    qh = [jnp.where(lane_head == e, q, jnp.zeros_like(q)) for e in range(SB_HPS)]
    u = u_ref[...]

    @pl.when(i == 0)
    def _():
        valid = _iota((T, T), 1) < _iota((T, T), 0)
        for e in range(SB_HPS):
            b = bias_ref[p * SB_HPS + e]
            bias_buf[e, 0] = jnp.where(valid, b, SB_MASKED)
            bias_buf[e, 1] = jnp.full((T, T), b, F32)
            bias_buf[e, 2] = jnp.full((T, T), SB_MASKED, F32)
        acc_ref[...] = jnp.zeros_like(acc_ref)
        carry_ref[...] = jnp.zeros_like(carry_ref)
        zs_buf[...] = jnp.zeros_like(zs_buf)
        cs_buf[...] = jnp.zeros_like(cs_buf)
        tot_buf[...] = jnp.zeros_like(tot_buf)

    def groups_of(qt):
        return _div(qt + SB_GROUP, SB_GROUP)

    def tile_of(qt, t):
        kj = qt - t
        kind = jnp.where(t == 0, 0, jnp.where(kj >= 0, 1, 2))
        return kind, pl.ds(pl.multiple_of(jnp.maximum(kj, 0) * T, T), T)

    def logits_stage(g):
        for j in range(SB_GROUP):
            kind, rows = tile_of(qi, g * SB_GROUP + j)
            kt = k_ref[rows, :]
            for e in range(SB_HPS):
                z = _dot_nt(qh[e], kt) + bias_buf[e, kind]
                sp = _softplus2(z)
                zs_buf[j, e] = z - sp
                cs = _dot(sp.astype(BF16), u)
                cs_buf[j, e] = cs
                tot_buf[j, e] = cs[:, 0:1] + sp[:, 0:1]

    def values_stage(qt, g):
        for j in range(SB_GROUP):
            _, rows = tile_of(qt, g * SB_GROUP + j)
            vt = v_ref[rows, :]
            for e in range(SB_HPS):
                w = jnp.exp2(zs_buf[j, e] - cs_buf[j, e])
                own = slice((e // SB_PAIR) * LANES, (e // SB_PAIR + 1) * LANES)
                acc_ref[e] += jnp.exp2(-carry_ref[e]) * _dot(w.astype(BF16), vt)[:, own]
                carry_ref[e] += tot_buf[j, e]

    values_stage(qprev, groups_of(qprev) - 1)

    half_head = _div(_iota((T, LANES), 1), SB_DH)
    for blk in range(SB_HPS // SB_PAIR):
        o = acc_ref[blk * SB_PAIR]
        for e in range(1, SB_PAIR):
            o = jnp.where(half_head == e, acc_ref[blk * SB_PAIR + e], o)
        sq = o * o
        ms = jnp.zeros_like(o)
        for e in range(SB_PAIR):
            ms_e = jnp.sum(jnp.where(half_head == e, sq, 0.0), axis=1, keepdims=True) * (1.0 / SB_DH)
            ms = jnp.where(half_head == e, ms_e, ms)
        lanes = slice(blk * LANES, (blk + 1) * LANES)
        o_ref[:, lanes] = (o * lax.rsqrt(ms + EPS) * norm_ref[:, lanes]).astype(BF16)

    acc_ref[...] = jnp.zeros_like(acc_ref)
    carry_ref[...] = jnp.zeros_like(carry_ref)
    logits_stage(0)

    def step(g, c):
        values_stage(qi, g - 1)
        logits_stage(g)
        return c

    lax.fori_loop(1, jnp.where(i < nq, groups_of(qi), 1), step, 0)


def _sb_prompt(bias, sq_b, sk_b, sv_b, norm_blk, batch, seq):
    nq = seq // SB_TILE
    nblk = SB_HEADS // SB_HPS
    j = np.arange(SB_TILE)[:, None]
    s = np.arange(SB_TILE)[None, :]
    u = jnp.asarray((j > s).astype(np.float32), BF16)
    return pl.pallas_call(
        _sb_prompt_body,
        grid=(batch, nblk, nq + 1),
        in_specs=[
            pl.BlockSpec(memory_space=pltpu.SMEM),
            pl.BlockSpec((SB_TILE, SB_BLK), lambda b, p, i: (b * nq + jnp.minimum(i, nq - 1), p)),
            pl.BlockSpec((seq, SB_BLK), lambda b, p, i: (b, p)),
            pl.BlockSpec((seq, SB_BLK), lambda b, p, i: (b, p)),
            pl.BlockSpec((1, SB_BLK), lambda b, p, i: (0, 0)),
            pl.BlockSpec((SB_TILE, SB_TILE), lambda b, p, i: (0, 0)),
        ],
        out_specs=pl.BlockSpec((SB_TILE, SB_BLK), lambda b, p, i: (b * nq + jnp.maximum(i - 1, 0), p)),
        out_shape=jax.ShapeDtypeStruct((batch * seq, SB_W), BF16),
        scratch_shapes=[
            pltpu.VMEM((SB_HPS, SB_TILE, LANES), F32),
            pltpu.VMEM((SB_HPS, SB_TILE, 1), F32),
            pltpu.VMEM((SB_HPS, 3, SB_TILE, SB_TILE), F32),
            pltpu.VMEM((SB_GROUP, SB_HPS, SB_TILE, SB_TILE), F32),
            pltpu.VMEM((SB_GROUP, SB_HPS, SB_TILE, SB_TILE), F32),
            pltpu.VMEM((SB_GROUP, SB_HPS, SB_TILE, 1), F32),
        ],
        compiler_params=pltpu.CompilerParams(dimension_semantics=("arbitrary", "arbitrary", "arbitrary"),
                                             vmem_limit_bytes=VMEM_LIMIT),
        name="sb_prompt",
    )(bias, sq_b, sk_b, sv_b, norm_blk, u)


SBS_PAR = 1


def _sb_scan_matrix():
    s_in = np.arange(LANES)[:, None]
    s_out = np.arange(LANES)[None, :]
    return np.concatenate([s_in > s_out, np.ones((LANES, LANES), bool)], axis=1).astype(np.float32)


def _sb_sample_seq(n_new, n_pages, bias, up, norm, q8, kn8, vn8, k_refs, v_refs):
    qrows = n_new * SB_HEADS
    own_head = _mod(_iota((qrows, SB_W), 0), SB_HEADS) == _div(_iota((qrows, SB_W), 1), SB_DH)
    q_rows = jnp.concatenate([jnp.broadcast_to(q8[t:t + 1, :], (SB_HEADS, SB_W)) for t in range(n_new)], axis=0)
    qbd = jnp.where(own_head, q_rows, 0.0).astype(BF16)

    zs = [_dot(qbd, k_refs[g][...].reshape(SB_W, PAGE_SIZE).astype(BF16)) + bias for g in range(n_pages)]
    pad = jnp.zeros((LANES - SUBLANES, SB_W), F32)
    kn = jnp.concatenate([kn8, pad], axis=0).astype(BF16)
    vn = jnp.concatenate([vn8, pad], axis=0).astype(BF16)
    zs.append(_dot_nt(qbd, kn) + bias)
    new_ok = _iota((qrows, LANES), 1) < _div(_iota((qrows, LANES), 0), SB_HEADS)
    sps = [_softplus2(z) for z in zs]
    sps[n_pages] = jnp.where(new_ok, sps[n_pages], 0.0)
    r = _dot(jnp.concatenate(sps, axis=0).astype(BF16), up)
    run = jnp.zeros((qrows, LANES), F32)
    acc = jnp.zeros((qrows, SB_W), F32)
    for g in reversed(range(n_pages + 1)):
        blk = r[g * qrows:(g + 1) * qrows]
        w = jnp.exp2(zs[g] - sps[g] - blk[:, 0:LANES] - run)
        run = run + blk[:, LANES:2 * LANES]
        if g == n_pages:
            acc = acc + _dot(jnp.where(new_ok, w, 0.0).astype(BF16), vn)
        else:
            acc = acc + _dot_nt(w.astype(BF16), v_refs[g][...].reshape(SB_W, PAGE_SIZE).astype(BF16))
    acc = jnp.where(own_head, acc, 0.0)
    o = acc[:, 0:SB_DH]
    for h in range(1, SB_HEADS):
        o = o + acc[:, h * SB_DH:(h + 1) * SB_DH]
    return _rms(o, norm).astype(BF16)


def _sb_sample_body(n_new, n_pages, n_par, pt_ref, bias_ref, q_ref, kn_ref, vn_ref, *rest):
    del pt_ref
    k_refs, v_refs = rest[:n_par * n_pages], rest[n_par * n_pages:2 * n_par * n_pages]
    norm_ref, up_ref, o_ref = rest[2 * n_par * n_pages:]
    for s in range(n_par):
        pages = slice(s * n_pages, (s + 1) * n_pages)
        o_ref[s] = _sb_sample_seq(n_new, n_pages, bias_ref[...], up_ref[...], norm_ref[...], q_ref[s], kn_ref[s],
                                  vn_ref[s], k_refs[pages], v_refs[pages])


def _sb_sample(page_table, bias_rows, q_s, k_new, v_new, pool_kt, pool_vt, norm, layer, n_new):
    n_seq, n_pages = page_table.shape
    qrows = n_new * SB_HEADS
    assert PAGE_SIZE == LANES and n_new <= SUBLANES
    up = jnp.asarray(_sb_scan_matrix(), BF16)
    n_par = SBS_PAR if n_seq % SBS_PAR == 0 else 1
    seq3 = lambda b, pt: (b, 0, 0)
    const = lambda b, pt: (0, 0)
    page_specs = [pl.BlockSpec((None, None, SB_HEADS, SB_DH, PAGE_SIZE),
                               lambda b, pt, s=s, g=g: (layer, pt[(b * n_par + s) * n_pages + g], 0, 0, 0))
                  for s in range(n_par) for g in range(n_pages)]
    grid_spec = pltpu.PrefetchScalarGridSpec(
        num_scalar_prefetch=1,
        grid=(n_seq // n_par,),
        in_specs=[
            pl.BlockSpec(bias_rows.shape, const),
            pl.BlockSpec((n_par, SUBLANES, SB_W), seq3),
            pl.BlockSpec((n_par, SUBLANES, SB_W), seq3),
            pl.BlockSpec((n_par, SUBLANES, SB_W), seq3),
            *page_specs, *page_specs,
            pl.BlockSpec((1, SB_DH), const),
            pl.BlockSpec(up.shape, const),
        ],
        out_specs=pl.BlockSpec((n_par, qrows, SB_DH), seq3),
    )
    return pl.pallas_call(
        functools.partial(_sb_sample_body, n_new, n_pages, n_par),
        grid_spec=grid_spec,
        out_shape=jax.ShapeDtypeStruct((n_seq, qrows, SB_DH), BF16),
        compiler_params=pltpu.CompilerParams(dimension_semantics=("arbitrary",),
                                             vmem_limit_bytes=VMEM_LIMIT),
        name="sb_sample",
    )(page_table.reshape(-1), bias_rows, q_s, k_new, v_new, *([pool_kt] * (n_par * n_pages)),
      *([pool_vt] * (n_par * n_pages)), norm, up)


def kernel(x_prompt, x_sample, cache_sb_k, cache_sb_v, state_gla, page_table, p_prompt, p_sample, f1_pre, f1_in, f1_out, f1_post, mix_pre, w_in, gate_w, gate_b, gla_norm, sb_norm, sb_bias, w_out, mix_post, f2_pre, f2_in, f2_out, f2_post, ple_pre, ple_w_gate, ple_w_up, ple_post):
    batch, seq, _ = x_prompt.shape
    n_seq, n_new, _ = x_sample.shape
    depth = f1_in.shape[0]
    n_p, n_s = batch * seq, n_seq * n_new
    assert n_p % TOK_TILE == 0 and n_s % TOK_TILE == 0 and seq % SB_TILE == 0 and n_p % GLA_S_ROWS == 0

    assert seq % TOK_TILE == 0
    x_p, x_s = x_prompt.reshape(n_p, D_MODEL), x_sample.reshape(n_s, D_MODEL)
    vec = lambda a: a.reshape(1, -1)
    lr0 = ZG_W
    sb0 = ZG_W + GLA_RANK
    pk, pv, ps, sk_l, sv_l, ss = [], [], [], [], [], []
    for l in range(depth):
        wcat = jnp.concatenate(
            [w_in[l][:, :lr0], w_in[l][:, sb0:], w_in[l][:, lr0:sb0],
             jnp.zeros((D_MODEL, LANES - GLA_RANK), F32)], axis=1).astype(BF16)
        gw_pad = jnp.concatenate([gate_w[l], jnp.zeros((LANES - GLA_RANK, GLA_QK_W), F32)], axis=0)
        x1, zg, la, sq_b, sk_b, sv_b, sk_t, sv_t, sk_s, sv_s = _ffn_in(
            x_p, x_s, batch, vec(f1_pre[l]), f1_in[l].astype(BF16), f1_out[l].astype(BF16), vec(f1_post[l]),
            vec(mix_pre[l]), wcat, gw_pad, vec(gate_b[l]))

        a_p, s_gla_p = _gla_prompt(zg, la, vec(gla_norm[l]), batch, seq)
        a_p = a_p.reshape(n_p, GLA_V_W)
        bias2 = sb_bias[l] * LOG2E
        o_sb_p = _sb_prompt(bias2, sq_b, sk_b, sv_b, vec(jnp.tile(sb_norm[l], SB_HPS)), batch, seq)

        a_s, s_gla_s = _gla_sample(zg, la, state_gla[l:l + 1], vec(gla_norm[l]), n_p, n_seq, n_new)
        rows8 = lambda a: jnp.pad(a.astype(F32).reshape(n_seq, n_new, SB_W),
                                  ((0, 0), (0, SUBLANES - n_new), (0, 0)))
        bias_rows = jnp.broadcast_to(jnp.tile(bias2, n_new)[:, None], (n_new * SB_HEADS, LANES))
        to_hdk = lambda c: jnp.transpose(c, (0, 1, 3, 4, 2))
        o_sb_s = _sb_sample(page_table, bias_rows, rows8(sq_b[n_p:]), rows8(sk_s), rows8(sv_s),
                            to_hdk(cache_sb_k), to_hdk(cache_sb_v), vec(sb_norm[l]), l, n_new)

        x_p, x_s = _merge_out(
            x1, a_p, a_s, o_sb_p, o_sb_s.reshape(n_s, SB_W), p_prompt[l].reshape(n_p, PLE_DIM),
            p_sample[l].reshape(n_s, PLE_DIM), w_out[l].astype(BF16), vec(mix_post[l]), vec(f2_pre[l]),
            f2_in[l].astype(BF16), f2_out[l].astype(BF16), vec(f2_post[l]), vec(ple_pre[l]),
            ple_w_gate[l].astype(BF16), ple_w_up[l].astype(BF16), vec(ple_post[l]))

        pk.append(jnp.transpose(sk_t, (0, 3, 1, 2)))
        pv.append(jnp.transpose(sv_t, (0, 3, 1, 2)))
        ps.append(s_gla_p)
        sk_l.append(sk_s.reshape(n_seq, n_new, SB_HEADS, SB_DH))
        sv_l.append(sv_s.reshape(n_seq, n_new, SB_HEADS, SB_DH))
        ss.append(s_gla_s[0])
    return (x_p.reshape(batch, seq, D_MODEL), x_s.reshape(n_seq, n_new, D_MODEL),
            jnp.stack(pk), jnp.stack(pv), jnp.stack(ps), jnp.stack(sk_l), jnp.stack(sv_l), jnp.stack(ss))
```

```python
import functools

import numpy as np
import jax
import jax.numpy as jnp
from jax import lax
from jax.experimental import pallas as pl
from jax.experimental.pallas import tpu as pltpu

F32 = jnp.float32
BF16 = jnp.bfloat16

D_MODEL = 1024
D_FF = 2816
PLE_DIM = 256
GLA_HEADS = 4
GLA_DK = 64
GLA_DV = 128
GLA_RANK = 16
GLA_GATE_TEMP = 16.0
GLA_CHUNK = 64
GLA_SUB = 64
SB_HEADS = 8
SB_DH = 64
PAGE_SIZE = 128
EPS = 1e-6
GLA_QK_W = GLA_HEADS * GLA_DK
GLA_V_W = GLA_HEADS * GLA_DV
SB_W = SB_HEADS * SB_DH
ZG_W = 2 * GLA_QK_W + 2 * GLA_V_W
LANES = 128
SUBLANES = 8
LOG2E = 1.4426950408889634

TOK_TILE = 512
FF_TILE = 256
VMEM_LIMIT = 62 * 1024 * 1024


def _log2(n):
    k = int(n).bit_length() - 1
    assert 1 << k == n, n
    return k


def _div(x, n):
    return lax.shift_right_logical(x, _log2(n))


def _mod(x, n):
    assert n & (n - 1) == 0
    return x & (n - 1)


def _iota(shape, dim):
    return lax.broadcasted_iota(jnp.int32, shape, dim)


def _dot(a, b):
    return jnp.dot(a, b, preferred_element_type=F32)


def _dot_nt(a, b):
    return lax.dot_general(a, b, (((1,), (1,)), ((), ())), preferred_element_type=F32)


def _dot01(t_bf16, x):
    hi = x.astype(BF16)
    lo = (x - hi.astype(F32)).astype(BF16)
    return _dot(t_bf16, hi) + _dot(t_bf16, lo)


def _rms(x, w):
    ms = jnp.mean(x * x, axis=-1, keepdims=True)
    return x * lax.rsqrt(ms + EPS) * w


def _sigmoid(x):
    return 1.0 / (1.0 + jnp.exp(-x))


def _softplus2(x):
    return jnp.maximum(x, jnp.log(1.0 + jnp.exp2(jnp.minimum(x, 126.0))) * LOG2E)


def _swiglu_ffn(h, win_ref, wout_ref, acc_ref):
    for j in range(D_FF // FF_TILE):
        g = _dot(h, win_ref[:, j * FF_TILE:(j + 1) * FF_TILE])
        u = _dot(h, win_ref[:, D_FF + j * FF_TILE:D_FF + (j + 1) * FF_TILE])
        act = (g * _sigmoid(g) * u).astype(BF16)
        part = _dot(act, wout_ref[j * FF_TILE:(j + 1) * FF_TILE, :])
        if j == 0:
            acc_ref[...] = part
        else:
            acc_ref[...] += part


def _resident(shape):
    return pl.BlockSpec(shape, lambda i: (0,) * len(shape), pipeline_mode=pl.Buffered(1))


def _ffn_in_body(n_ptiles, xp_ref, xs_ref, pre_ref, win_ref, wout_ref, post_ref, mixpre_ref, wcat_ref, gw_ref,
                 gb_ref, x1_ref, zg_ref, la_ref, sq_ref, skb_ref, svb_ref, skt_ref, svt_ref, sks_ref, svs_ref,
                 acc_ref):
    is_prompt = pl.program_id(0) < n_ptiles
    x = jnp.where(is_prompt, xp_ref[...], xs_ref[...])
    _swiglu_ffn(_rms(x, pre_ref[...]).astype(BF16), win_ref, wout_ref, acc_ref)
    x1 = x + 0.5 * _rms(acc_ref[...], post_ref[...])
    x1_ref[...] = x1
    h2 = _rms(x1, mixpre_ref[...]).astype(BF16)
    sb = _dot(h2, wcat_ref[:, ZG_W:ZG_W + 3 * SB_W])
    sq_ref[...] = (sb[:, 0:SB_W] * (SB_DH ** -0.5 * LOG2E)).astype(BF16)
    sk = sb[:, SB_W:2 * SB_W]
    sv = sb[:, 2 * SB_W:3 * SB_W]
    skb_ref[...] = sk.astype(BF16)
    svb_ref[...] = sv.astype(BF16)
    sks_ref[...] = sk
    svs_ref[...] = sv
    sk_t = sk.T.reshape(SB_HEADS, SB_DH, TOK_TILE)
    sv_t = sv.T.reshape(SB_HEADS, SB_DH, TOK_TILE)
    lr = _dot(h2, wcat_ref[:, ZG_W + 3 * SB_W:])
    logit = jnp.dot(lr, gw_ref[...], precision=lax.Precision.HIGHEST,
                    preferred_element_type=F32) + gb_ref[...]
    la_ref[...] = (jnp.minimum(logit, 0.0) - jnp.log1p(jnp.exp(-jnp.abs(logit)))) * (1.0 / GLA_GATE_TEMP)
    zg_ref[...] = _dot(h2, wcat_ref[:, 0:ZG_W])

    @pl.when(is_prompt)
    def _():
        skt_ref[0] = sk_t
        svt_ref[0] = sv_t


def _ffn_in(x_p, x_s, batch, pre, w_in_b, w_out_b, post, mixpre, wcat_b, gw_pad, gb):
    n_p, n_s = x_p.shape[0], x_s.shape[0]
    n = n_p + n_s
    seq = n_p // batch
    n_pt, per_seq = n_p // TOK_TILE, seq // TOK_TILE
    row = lambda i: (i, 0)
    p_row = lambda i: (jnp.minimum(i, n_pt - 1), 0)
    s_row = lambda i: (jnp.maximum(i - n_pt, 0), 0)
    p_kv = lambda i: (jnp.minimum(i, n_pt - 1) // per_seq, 0, 0, jnp.minimum(i, n_pt - 1) % per_seq)
    vecd = _resident((1, D_MODEL))
    out_shapes = (
        jax.ShapeDtypeStruct((n, D_MODEL), F32),
        jax.ShapeDtypeStruct((n, ZG_W), F32),
        jax.ShapeDtypeStruct((n, GLA_QK_W), F32),
        jax.ShapeDtypeStruct((n, SB_W), BF16),
        jax.ShapeDtypeStruct((n, SB_W), BF16),
        jax.ShapeDtypeStruct((n, SB_W), BF16),
        jax.ShapeDtypeStruct((batch, SB_HEADS, SB_DH, seq), F32),
        jax.ShapeDtypeStruct((batch, SB_HEADS, SB_DH, seq), F32),
        jax.ShapeDtypeStruct((n_s, SB_W), F32),
        jax.ShapeDtypeStruct((n_s, SB_W), F32),
    )
    return pl.pallas_call(
        functools.partial(_ffn_in_body, n_pt),
        grid=(n // TOK_TILE,),
        in_specs=[
            pl.BlockSpec((TOK_TILE, D_MODEL), p_row),
            pl.BlockSpec((TOK_TILE, D_MODEL), s_row),
            vecd,
            _resident(w_in_b.shape),
            _resident(w_out_b.shape),
            vecd, vecd,
            _resident(wcat_b.shape),
            _resident(gw_pad.shape),
            _resident((1, GLA_QK_W)),
        ],
        out_specs=[
            pl.BlockSpec((TOK_TILE, D_MODEL), row),
            pl.BlockSpec((TOK_TILE, ZG_W), row),
            pl.BlockSpec((TOK_TILE, GLA_QK_W), row),
            pl.BlockSpec((TOK_TILE, SB_W), row),
            pl.BlockSpec((TOK_TILE, SB_W), row),
            pl.BlockSpec((TOK_TILE, SB_W), row),
            pl.BlockSpec((1, SB_HEADS, SB_DH, TOK_TILE), p_kv),
            pl.BlockSpec((1, SB_HEADS, SB_DH, TOK_TILE), p_kv),
            pl.BlockSpec((TOK_TILE, SB_W), s_row),
            pl.BlockSpec((TOK_TILE, SB_W), s_row),
        ],
        out_shape=out_shapes,
        scratch_shapes=[pltpu.VMEM((TOK_TILE, D_MODEL), F32)],
        compiler_params=pltpu.CompilerParams(dimension_semantics=("arbitrary",),
                                             vmem_limit_bytes=VMEM_LIMIT),
        name="ffn_in",
    )(x_p, x_s, pre, w_in_b, w_out_b, post, mixpre, wcat_b, gw_pad, gb)


def _merge_out_body(n_ptiles, x1_ref, ap_ref, as_ref, sp_ref, ss_ref, pp_ref, ps_ref, wout_ref, mixpost_ref,
                    pre_ref, win_ref, woutf_ref, post_ref, plepre_ref, pleg_ref, pleu_ref, plepost_ref,
                    yp_ref, ys_ref, acc_ref):
    is_prompt = pl.program_id(0) < n_ptiles
    a = jnp.where(is_prompt, ap_ref[...], as_ref[...])
    s = jnp.where(is_prompt, sp_ref[...], ss_ref[...])
    p = jnp.where(is_prompt, pp_ref[...], ps_ref[...])
    m = _dot(a, wout_ref[0:GLA_V_W, :]) + _dot(s, wout_ref[GLA_V_W:, :])
    x2 = x1_ref[...] + _rms(m, mixpost_ref[...])
    _swiglu_ffn(_rms(x2, pre_ref[...]).astype(BF16), win_ref, woutf_ref, acc_ref)
    x3 = x2 + 0.5 * _rms(acc_ref[...], post_ref[...])
    hg = _rms(x3, plepre_ref[...]).astype(BF16)
    gate = _sigmoid(_dot(hg, pleg_ref[...]))
    e = _dot(p.astype(BF16), pleu_ref[...])
    y = x3 + _rms(gate * e, plepost_ref[...])

    @pl.when(is_prompt)
    def _():
        yp_ref[...] = y

    @pl.when(jnp.logical_not(is_prompt))
    def _():
        ys_ref[...] = y


def _merge_out(x1, a_p, a_s, s_p, s_s, p_p, p_s, wout_b, mixpost, pre, w_in_b, w_out_b, post, plepre, pleg_b,
               pleu_b, plepost):
    n_p, n_s = a_p.shape[0], a_s.shape[0]
    n_pt = n_p // TOK_TILE
    row = lambda i: (i, 0)
    p_row = lambda i: (jnp.minimum(i, n_pt - 1), 0)
    s_row = lambda i: (jnp.maximum(i - n_pt, 0), 0)
    pair = lambda width: [pl.BlockSpec((TOK_TILE, width), p_row), pl.BlockSpec((TOK_TILE, width), s_row)]
    vecd = _resident((1, D_MODEL))
    return pl.pallas_call(
        functools.partial(_merge_out_body, n_pt),
        grid=((n_p + n_s) // TOK_TILE,),
        in_specs=[
            pl.BlockSpec((TOK_TILE, D_MODEL), row),
            *pair(GLA_V_W), *pair(SB_W), *pair(PLE_DIM),
            _resident(wout_b.shape),
            vecd, vecd,
            _resident(w_in_b.shape),
            _resident(w_out_b.shape),
            vecd, vecd,
            _resident(pleg_b.shape),
            _resident(pleu_b.shape),
            vecd,
        ],
        out_specs=pair(D_MODEL),
        out_shape=(jax.ShapeDtypeStruct((n_p, D_MODEL), F32), jax.ShapeDtypeStruct((n_s, D_MODEL), F32)),
        scratch_shapes=[pltpu.VMEM((TOK_TILE, D_MODEL), F32)],
        compiler_params=pltpu.CompilerParams(dimension_semantics=("arbitrary",),
                                             vmem_limit_bytes=VMEM_LIMIT),
        name="merge_out",
    )(x1, a_p, a_s, s_p, s_s, p_p, p_s, wout_b, mixpost, pre, w_in_b, w_out_b, post, plepre, pleg_b, pleu_b,
      plepost)


GLA_BLOCK = 256
GLA_PAIR = 128
GLA_PAR = 4


def _halving_levels(block):
    return [block >> i for i in range(_log2(block))]


def _cum_matrices(block, chunk, sub=None, levels=()):
    t = np.arange(block)[:, None]
    s = np.arange(block)[None, :]
    same = (t // chunk) == (s // chunk)
    mats = [same & (s <= t)]
    if sub is not None:
        mats.append(same & ((s % chunk) < sub * ((t % chunk) // sub)))
    mats.append(same)
    for lvl in levels:
        mats.append(same & ((s % chunk) <= lvl * ((t % chunk) // lvl) + lvl // 2 - 1))
    return np.concatenate(mats, axis=0).astype(np.float32)


def _gla_block_scores(q, k, b, mids, levels, head_sel):
    rows = q.shape[0]
    hr = GLA_HEADS * rows
    t = _mod(_iota((hr, rows), 0), rows)
    s = _iota((hr, rows), 1)
    t_row = _iota((rows, GLA_QK_W), 0)
    stack = lambda a: jnp.where(head_sel, jnp.concatenate([a] * GLA_HEADS, axis=0), 0.0).astype(BF16)
    att = jnp.where(t == s, _dot_nt(stack(q), k.astype(BF16)), 0.0)
    for lvl, mid in zip(levels, mids):
        second = _mod(t_row, lvl) >= lvl // 2
        q_l = jnp.where(second, q * jnp.exp(jnp.minimum(b - mid, 0.0)), 0.0)
        k_l = jnp.where(second, 0.0, k * jnp.exp(jnp.minimum(mid - b, 0.0)))
        att = att + jnp.where(_div(t, lvl) == _div(s, lvl), _dot_nt(stack(q_l), k_l.astype(BF16)), 0.0)
    return att


def _gla_seq_block(bb, zg_ref, la_ref, norm_ref, cum_ref, a_ref, st_ref):
    nb = GLA_BLOCK
    levels = _halving_levels(GLA_SUB)
    cums = _dot01(cum_ref[...], la_ref[...])
    b = cums[0:nb]
    r = cums[nb:2 * nb]
    bl = cums[2 * nb:3 * nb]
    mids = [cums[(3 + n) * nb:(4 + n) * nb] for n in range(len(levels))]
    q = zg_ref[:, 0:GLA_QK_W] * GLA_DK ** -0.5
    k = zg_ref[:, GLA_QK_W:2 * GLA_QK_W]
    q_intra = q * jnp.exp(b - r)
    q_inter = q * jnp.exp(b)
    k_state = (k * jnp.exp(bl - b)).astype(BF16)
    decay = jnp.exp(bl)

    C, S = GLA_CHUNK, GLA_SUB
    n_sub = C // S
    hc = GLA_HEADS * C
    head_sel = _div(_iota((hc, GLA_QK_W), 1), GLA_DK) == _div(_iota((hc, GLA_QK_W), 0), C)
    tt = _mod(_iota((hc, n_sub * C), 0), C)
    cc = _iota((hc, n_sub * C), 1)
    att_keep = (_div(cc, C) == _div(tt, S)) & (_mod(cc, C) < S * _div(tt, S))
    srow = _iota((C, GLA_QK_W), 0)
    pair_chunk = _div(_iota((GLA_DV, GLA_PAIR), 1), C)
    st_lane_head = _div(_iota((GLA_DV, GLA_QK_W), 1), GLA_DK)

    for c in range(nb // C):
        rows = slice(c * C, (c + 1) * C)
        pair = slice((c // 2) * GLA_PAIR, (c // 2 + 1) * GLA_PAIR)
        b_c, k_c = b[rows], k[rows]
        if n_sub > 1:
            kst = []
            for sidx in range(n_sub):
                r_row = r[c * C + sidx * S:c * C + sidx * S + 1, :]
                kk = k_c * jnp.exp(jnp.minimum(r_row - b_c, 0.0))
                kst.append(jnp.where(srow < sidx * S, kk, 0.0))
            kst = jnp.concatenate(kst, axis=0).astype(BF16)
            qst = jnp.where(head_sel, jnp.concatenate([q_intra[rows]] * GLA_HEADS, axis=0), 0.0).astype(BF16)
            att = jnp.where(att_keep, _dot_nt(qst, kst), 0.0).astype(BF16)
        att_in = _gla_block_scores(q[rows], k_c, b_c, [m[rows] for m in mids], levels, head_sel).astype(BF16)
        qin = jnp.where(head_sel, jnp.concatenate([q_inter[rows]] * GLA_HEADS, axis=0), 0.0).astype(BF16)
        o_inter = _dot_nt(qin, st_ref[bb].astype(BF16))
        upd = jnp.zeros((GLA_DV, GLA_QK_W), F32)
        for h in range(GLA_HEADS):
            vcol = slice(2 * GLA_QK_W + h * GLA_DV, 2 * GLA_QK_W + (h + 1) * GLA_DV)
            gcol = slice(2 * GLA_QK_W + GLA_V_W + h * GLA_DV, 2 * GLA_QK_W + GLA_V_W + (h + 1) * GLA_DV)
            v_h = zg_ref[rows, vcol].astype(BF16)
            o = _dot(att_in[h * C:(h + 1) * C], v_h) + o_inter[h * C:(h + 1) * C]
            if n_sub > 1:
                o = o + _dot(att[h * C:(h + 1) * C], jnp.concatenate([v_h] * n_sub, axis=0))
            gate = zg_ref[rows, gcol]
            a_ref[bb, rows, h * GLA_DV:(h + 1) * GLA_DV] = (
                _rms(o, norm_ref[...]) * (gate * _sigmoid(gate))).astype(BF16)
            v_t = zg_ref[pair, vcol].T
            u_t = _dot(jnp.where(pair_chunk == (c % 2), v_t, 0.0).astype(BF16), k_state[pair])
            upd = upd + jnp.where(st_lane_head == h, u_t, 0.0)
        st_ref[bb] = st_ref[bb] * decay[c * C:c * C + 1, :] + upd


def _gla_prompt_body(n_par, *refs):
    zg_refs, la_refs = refs[:n_par], refs[n_par:2 * n_par]
    norm_ref, cum_ref, a_ref, sout_ref, st_ref = refs[2 * n_par:]
    i = pl.program_id(1)

    @pl.when(i == 0)
    def _():
        st_ref[...] = jnp.zeros_like(st_ref)

    for bb in range(n_par):
        _gla_seq_block(bb, zg_refs[bb], la_refs[bb], norm_ref, cum_ref, a_ref, st_ref)

    @pl.when(i == pl.num_programs(1) - 1)
    def _():
        for bb in range(n_par):
            sout_ref[bb] = st_ref[bb].T.reshape(GLA_HEADS, GLA_DK, GLA_DV)


def _gla_prompt(zg, la, norm, batch, seq):
    n_par = GLA_PAR if batch % GLA_PAR == 0 else 1
    nblk = seq // GLA_BLOCK
    cum = jnp.asarray(_cum_matrices(GLA_BLOCK, GLA_CHUNK, GLA_SUB, _halving_levels(GLA_SUB)), BF16)
    rows = [lambda g, i, bb=bb: ((g * n_par + bb) * nblk + i, 0) for bb in range(n_par)]
    return pl.pallas_call(
        functools.partial(_gla_prompt_body, n_par),
        grid=(batch // n_par, nblk),
        in_specs=[
            *[pl.BlockSpec((GLA_BLOCK, ZG_W), r) for r in rows],
            *[pl.BlockSpec((GLA_BLOCK, GLA_QK_W), r) for r in rows],
            pl.BlockSpec((1, GLA_DV), lambda g, i: (0, 0)),
            pl.BlockSpec(cum.shape, lambda g, i: (0, 0)),
        ],
        out_specs=[
            pl.BlockSpec((n_par, GLA_BLOCK, GLA_V_W), lambda g, i: (g, i, 0)),
            pl.BlockSpec((n_par, GLA_HEADS, GLA_DK, GLA_DV), lambda g, i: (g, 0, 0, 0)),
        ],
        out_shape=(jax.ShapeDtypeStruct((batch, seq, GLA_V_W), BF16),
                   jax.ShapeDtypeStruct((batch, GLA_HEADS, GLA_DK, GLA_DV), F32)),
        scratch_shapes=[pltpu.VMEM((n_par, GLA_DV, GLA_QK_W), F32)],
        compiler_params=pltpu.CompilerParams(dimension_semantics=("arbitrary", "arbitrary"),
                                             vmem_limit_bytes=VMEM_LIMIT),
        name="gla_prompt",
    )(*([zg] * n_par), *([la] * n_par), norm, cum)


GLA_S_ROWS = 128


def _gla_sample_body(n_new, zg_ref, la_ref, s_ref, norm_ref, cum_ref, a_ref, sout_ref,
                     qin_ref, khat_ref, dec_ref, oint_ref, vt_ref):
    rows = GLA_S_ROWS
    levels = _halving_levels(n_new)
    cums = _dot01(cum_ref[...], la_ref[...])
    b = cums[0:rows]
    bl = cums[rows:2 * rows]
    mids = [cums[(2 + n) * rows:(3 + n) * rows] for n in range(len(levels))]
    q = zg_ref[:, 0:GLA_QK_W] * GLA_DK ** -0.5
    k = zg_ref[:, GLA_QK_W:2 * GLA_QK_W]
    qin_ref[...] = q * jnp.exp(b)
    khat_ref[...] = (k * jnp.exp(bl - b)).astype(BF16)
    dec_ref[...] = jnp.exp(bl)

    hr = GLA_HEADS * rows
    head_sel = _div(_iota((hr, GLA_QK_W), 1), GLA_DK) == _div(_iota((hr, GLA_QK_W), 0), rows)
    att = _gla_block_scores(q, k, b, mids, levels, head_sel).astype(BF16)
    for h in range(GLA_HEADS):
        vcol = slice(2 * GLA_QK_W + h * GLA_DV, 2 * GLA_QK_W + (h + 1) * GLA_DV)
        v_h = zg_ref[:, vcol]
        oint_ref[:, h * GLA_DV:(h + 1) * GLA_DV] = _dot(att[h * rows:(h + 1) * rows], v_h.astype(BF16))
        vt_ref[h] = v_h.T

    per8 = SUBLANES // n_new
    sub8 = _div(_iota((SUBLANES, GLA_QK_W), 0), n_new)
    lane8 = _div(_iota((SUBLANES, GLA_QK_W), 1), GLA_DK)
    out8 = _div(_iota((SUBLANES, GLA_DV), 0), n_new)
    vt_seq = _div(_iota((GLA_DV, rows), 1), n_new)
    st_lane_head = _div(_iota((GLA_DV, GLA_QK_W), 1), GLA_DK)

    def group_step(m, carry):
        r8 = pl.ds(pl.multiple_of(m * SUBLANES, SUBLANES), SUBLANES)
        q8 = qin_ref[r8, :]
        d8 = dec_ref[r8, :]
        res = []
        for e in range(per8):
            n = m * per8 + e
            s_n = s_ref[0, n].reshape(GLA_HEADS * GLA_DK, GLA_DV)
            lhs = jnp.concatenate([jnp.where((sub8 == e) & (lane8 == h), q8, 0.0) for h in range(GLA_HEADS)],
                                  axis=0)
            res.append(_dot(lhs.astype(BF16), s_n.astype(BF16)))
            upd = jnp.zeros((GLA_DV, GLA_QK_W), F32)
            for h in range(GLA_HEADS):
                u_t = _dot(jnp.where(vt_seq == n, vt_ref[h], 0.0).astype(BF16), khat_ref[...])
                upd = upd + jnp.where(st_lane_head == h, u_t, 0.0)
            s_new_t = s_n.T * d8[e * n_new:e * n_new + 1, :] + upd
            sout_ref[0, n] = s_new_t.T.reshape(GLA_HEADS, GLA_DK, GLA_DV)
        for h in range(GLA_HEADS):
            o8 = res[0][h * SUBLANES:(h + 1) * SUBLANES]
            for e in range(1, per8):
                o8 = jnp.where(out8 == e, res[e][h * SUBLANES:(h + 1) * SUBLANES], o8)
            oint_ref[r8, h * GLA_DV:(h + 1) * GLA_DV] += o8
        return carry

    lax.fori_loop(0, rows // SUBLANES, group_step, 0)

    for h in range(GLA_HEADS):
        gcol = slice(2 * GLA_QK_W + GLA_V_W + h * GLA_DV, 2 * GLA_QK_W + GLA_V_W + (h + 1) * GLA_DV)
        gate = zg_ref[:, gcol]
        o = oint_ref[:, h * GLA_DV:(h + 1) * GLA_DV]
        a_ref[:, h * GLA_DV:(h + 1) * GLA_DV] = (_rms(o, norm_ref[...]) * (gate * _sigmoid(gate))).astype(BF16)


def _gla_sample(zg, la, state, norm, row0, n_seq, n_new):
    assert SUBLANES % n_new == 0 and GLA_S_ROWS % n_new == 0
    seqs = GLA_S_ROWS // n_new
    steps = n_seq // seqs
    blk0 = row0 // GLA_S_ROWS
    cum = jnp.asarray(_cum_matrices(GLA_S_ROWS, n_new, levels=_halving_levels(n_new)), BF16)
    return pl.pallas_call(
        functools.partial(_gla_sample_body, n_new),
        grid=(steps,),
        in_specs=[
            pl.BlockSpec((GLA_S_ROWS, ZG_W), lambda i: (blk0 + i, 0)),
            pl.BlockSpec((GLA_S_ROWS, GLA_QK_W), lambda i: (blk0 + i, 0)),
            pl.BlockSpec((1, seqs, GLA_HEADS, GLA_DK, GLA_DV), lambda i: (0, i, 0, 0, 0)),
            pl.BlockSpec((1, GLA_DV), lambda i: (0, 0)),
            pl.BlockSpec(cum.shape, lambda i: (0, 0)),
        ],
        out_specs=[
            pl.BlockSpec((GLA_S_ROWS, GLA_V_W), lambda i: (i, 0)),
            pl.BlockSpec((1, seqs, GLA_HEADS, GLA_DK, GLA_DV), lambda i: (0, i, 0, 0, 0)),
        ],
        out_shape=(jax.ShapeDtypeStruct((n_seq * n_new, GLA_V_W), BF16),
                   jax.ShapeDtypeStruct((1, n_seq, GLA_HEADS, GLA_DK, GLA_DV), F32)),
        scratch_shapes=[
            pltpu.VMEM((GLA_S_ROWS, GLA_QK_W), F32),
            pltpu.VMEM((GLA_S_ROWS, GLA_QK_W), BF16),
            pltpu.VMEM((GLA_S_ROWS, GLA_QK_W), F32),
            pltpu.VMEM((GLA_S_ROWS, GLA_V_W), F32),
            pltpu.VMEM((GLA_HEADS, GLA_DV, GLA_S_ROWS), F32),
        ],
        compiler_params=pltpu.CompilerParams(dimension_semantics=("arbitrary",),
                                             vmem_limit_bytes=VMEM_LIMIT),
        name="gla_sample",
    )(zg, la, state, norm, cum)


SB_TILE = 256
SB_PAIR = LANES // SB_DH
SB_HPS = 4
SB_BLK = SB_HPS * SB_DH
SB_GROUP = 2
SB_MASKED = -1e30


def _sb_prompt_body(bias_ref, q_ref, k_ref, v_ref, norm_ref, u_ref, o_ref, acc_ref, carry_ref, bias_buf, zs_buf,
                    cs_buf, tot_buf):
    p = pl.program_id(1)
    i = pl.program_id(2)
    nq = pl.num_programs(2) - 1
    qi = jnp.minimum(i, nq - 1)
    qprev = jnp.maximum(i - 1, 0)
    T = SB_TILE
    lane_head = _div(_iota((T, SB_BLK), 1), SB_DH)
    q = q_ref[...]
    qh = [jnp.where(lane_head == e, q, jnp.zeros_like(q)) for e in range(SB_HPS)]
    u = u_ref[...]

    @pl.when(i == 0)
    def _():
        valid = _iota((T, T), 1) < _iota((T, T), 0)
        for e in range(SB_HPS):
            b = bias_ref[p * SB_HPS + e]
            bias_buf[e, 0] = jnp.where(valid, b, SB_MASKED)
            bias_buf[e, 1] = jnp.full((T, T), b, F32)
            bias_buf[e, 2] = jnp.full((T, T), SB_MASKED, F32)
        acc_ref[...] = jnp.zeros_like(acc_ref)
        carry_ref[...] = jnp.zeros_like(carry_ref)
        zs_buf[...] = jnp.zeros_like(zs_buf)
        cs_buf[...] = jnp.zeros_like(cs_buf)
        tot_buf[...] = jnp.zeros_like(tot_buf)

    def groups_of(qt):
        return _div(qt + SB_GROUP, SB_GROUP)

    def tile_of(qt, t):
        kj = qt - t
        kind = jnp.where(t == 0, 0, jnp.where(kj >= 0, 1, 2))
        return kind, pl.ds(pl.multiple_of(jnp.maximum(kj, 0) * T, T), T)

    def logits_stage(g):
        for j in range(SB_GROUP):
            kind, rows = tile_of(qi, g * SB_GROUP + j)
            kt = k_ref[rows, :]
            for e in range(SB_HPS):
                z = _dot_nt(qh[e], kt) + bias_buf[e, kind]
                sp = _softplus2(z)
                zs_buf[j, e] = z - sp
                cs = _dot(sp.astype(BF16), u)
                cs_buf[j, e] = cs
                tot_buf[j, e] = cs[:, 0:1] + sp[:, 0:1]

    def values_stage(qt, g):
        for j in range(SB_GROUP):
            _, rows = tile_of(qt, g * SB_GROUP + j)
            vt = v_ref[rows, :]
            for e in range(SB_HPS):
                w = jnp.exp2(zs_buf[j, e] - cs_buf[j, e])
                own = slice((e // SB_PAIR) * LANES, (e // SB_PAIR + 1) * LANES)
                acc_ref[e] += jnp.exp2(-carry_ref[e]) * _dot(w.astype(BF16), vt)[:, own]
                carry_ref[e] += tot_buf[j, e]

    values_stage(qprev, groups_of(qprev) - 1)

    half_head = _div(_iota((T, LANES), 1), SB_DH)
    for blk in range(SB_HPS // SB_PAIR):
        o = acc_ref[blk * SB_PAIR]
        for e in range(1, SB_PAIR):
            o = jnp.where(half_head == e, acc_ref[blk * SB_PAIR + e], o)
        sq = o * o
        ms = jnp.zeros_like(o)
        for e in range(SB_PAIR):
            ms_e = jnp.sum(jnp.where(half_head == e, sq, 0.0), axis=1, keepdims=True) * (1.0 / SB_DH)
            ms = jnp.where(half_head == e, ms_e, ms)
        lanes = slice(blk * LANES, (blk + 1) * LANES)
        o_ref[:, lanes] = (o * lax.rsqrt(ms + EPS) * norm_ref[:, lanes]).astype(BF16)

    acc_ref[...] = jnp.zeros_like(acc_ref)
    carry_ref[...] = jnp.zeros_like(carry_ref)
    logits_stage(0)

    def step(g, c):
        values_stage(qi, g - 1)
        logits_stage(g)
        return c

    lax.fori_loop(1, jnp.where(i < nq, groups_of(qi), 1), step, 0)


def _sb_prompt(bias, sq_b, sk_b, sv_b, norm_blk, batch, seq):
    nq = seq // SB_TILE
    nblk = SB_HEADS // SB_HPS
    j = np.arange(SB_TILE)[:, None]
    s = np.arange(SB_TILE)[None, :]
    u = jnp.asarray((j > s).astype(np.float32), BF16)
    return pl.pallas_call(
        _sb_prompt_body,
        grid=(batch, nblk, nq + 1),
        in_specs=[
            pl.BlockSpec(memory_space=pltpu.SMEM),
            pl.BlockSpec((SB_TILE, SB_BLK), lambda b, p, i: (b * nq + jnp.minimum(i, nq - 1), p)),
            pl.BlockSpec((seq, SB_BLK), lambda b, p, i: (b, p)),
            pl.BlockSpec((seq, SB_BLK), lambda b, p, i: (b, p)),
            pl.BlockSpec((1, SB_BLK), lambda b, p, i: (0, 0)),
            pl.BlockSpec((SB_TILE, SB_TILE), lambda b, p, i: (0, 0)),
        ],
        out_specs=pl.BlockSpec((SB_TILE, SB_BLK), lambda b, p, i: (b * nq + jnp.maximum(i - 1, 0), p)),
        out_shape=jax.ShapeDtypeStruct((batch * seq, SB_W), BF16),
        scratch_shapes=[
            pltpu.VMEM((SB_HPS, SB_TILE, LANES), F32),
            pltpu.VMEM((SB_HPS, SB_TILE, 1), F32),
            pltpu.VMEM((SB_HPS, 3, SB_TILE, SB_TILE), F32),
            pltpu.VMEM((SB_GROUP, SB_HPS, SB_TILE, SB_TILE), F32),
            pltpu.VMEM((SB_GROUP, SB_HPS, SB_TILE, SB_TILE), F32),
            pltpu.VMEM((SB_GROUP, SB_HPS, SB_TILE, 1), F32),
        ],
        compiler_params=pltpu.CompilerParams(dimension_semantics=("arbitrary", "arbitrary", "arbitrary"),
                                             vmem_limit_bytes=VMEM_LIMIT),
        name="sb_prompt",
    )(bias, sq_b, sk_b, sv_b, norm_blk, u)


SBS_PAR = 1


def _sb_scan_matrix():
    s_in = np.arange(LANES)[:, None]
    s_out = np.arange(LANES)[None, :]
    return np.concatenate([s_in > s_out, np.ones((LANES, LANES), bool)], axis=1).astype(np.float32)


def _sb_sample_seq(n_new, n_pages, bias, up, norm, q8, kn8, vn8, k_refs, v_refs):
    qrows = n_new * SB_HEADS
    own_head = _mod(_iota((qrows, SB_W), 0), SB_HEADS) == _div(_iota((qrows, SB_W), 1), SB_DH)
    q_rows = jnp.concatenate([jnp.broadcast_to(q8[t:t + 1, :], (SB_HEADS, SB_W)) for t in range(n_new)], axis=0)
    qbd = jnp.where(own_head, q_rows, 0.0).astype(BF16)

    zs = [_dot(qbd, k_refs[g][...].reshape(SB_W, PAGE_SIZE).astype(BF16)) + bias for g in range(n_pages)]
    pad = jnp.zeros((LANES - SUBLANES, SB_W), F32)
    kn = jnp.concatenate([kn8, pad], axis=0).astype(BF16)
    vn = jnp.concatenate([vn8, pad], axis=0).astype(BF16)
    zs.append(_dot_nt(qbd, kn) + bias)
    new_ok = _iota((qrows, LANES), 1) < _div(_iota((qrows, LANES), 0), SB_HEADS)
    sps = [_softplus2(z) for z in zs]
    sps[n_pages] = jnp.where(new_ok, sps[n_pages], 0.0)
    r = _dot(jnp.concatenate(sps, axis=0).astype(BF16), up)
    run = jnp.zeros((qrows, LANES), F32)
    acc = jnp.zeros((qrows, SB_W), F32)
    for g in reversed(range(n_pages + 1)):
        blk = r[g * qrows:(g + 1) * qrows]
        w = jnp.exp2(zs[g] - sps[g] - blk[:, 0:LANES] - run)
        run = run + blk[:, LANES:2 * LANES]
        if g == n_pages:
            acc = acc + _dot(jnp.where(new_ok, w, 0.0).astype(BF16), vn)
        else:
            acc = acc + _dot_nt(w.astype(BF16), v_refs[g][...].reshape(SB_W, PAGE_SIZE).astype(BF16))
    acc = jnp.where(own_head, acc, 0.0)
    o = acc[:, 0:SB_DH]
    for h in range(1, SB_HEADS):
        o = o + acc[:, h * SB_DH:(h + 1) * SB_DH]
    return _rms(o, norm).astype(BF16)


def _sb_sample_body(n_new, n_pages, n_par, pt_ref, bias_ref, q_ref, kn_ref, vn_ref, *rest):
    del pt_ref
    k_refs, v_refs = rest[:n_par * n_pages], rest[n_par * n_pages:2 * n_par * n_pages]
    norm_ref, up_ref, o_ref = rest[2 * n_par * n_pages:]
    for s in range(n_par):
        pages = slice(s * n_pages, (s + 1) * n_pages)
        o_ref[s] = _sb_sample_seq(n_new, n_pages, bias_ref[...], up_ref[...], norm_ref[...], q_ref[s], kn_ref[s],
                                  vn_ref[s], k_refs[pages], v_refs[pages])


def _sb_sample(page_table, bias_rows, q_s, k_new, v_new, pool_kt, pool_vt, norm, layer, n_new):
    n_seq, n_pages = page_table.shape
    qrows = n_new * SB_HEADS
    assert PAGE_SIZE == LANES and n_new <= SUBLANES
    up = jnp.asarray(_sb_scan_matrix(), BF16)
    n_par = SBS_PAR if n_seq % SBS_PAR == 0 else 1
    seq3 = lambda b, pt: (b, 0, 0)
    const = lambda b, pt: (0, 0)
    page_specs = [pl.BlockSpec((None, None, SB_HEADS, SB_DH, PAGE_SIZE),
                               lambda b, pt, s=s, g=g: (layer, pt[(b * n_par + s) * n_pages + g], 0, 0, 0))
                  for s in range(n_par) for g in range(n_pages)]
    grid_spec = pltpu.PrefetchScalarGridSpec(
        num_scalar_prefetch=1,
        grid=(n_seq // n_par,),
        in_specs=[
            pl.BlockSpec(bias_rows.shape, const),
            pl.BlockSpec((n_par, SUBLANES, SB_W), seq3),
            pl.BlockSpec((n_par, SUBLANES, SB_W), seq3),
            pl.BlockSpec((n_par, SUBLANES, SB_W), seq3),
            *page_specs, *page_specs,
            pl.BlockSpec((1, SB_DH), const),
            pl.BlockSpec(up.shape, const),
        ],
        out_specs=pl.BlockSpec((n_par, qrows, SB_DH), seq3),
    )
    return pl.pallas_call(
        functools.partial(_sb_sample_body, n_new, n_pages, n_par),
        grid_spec=grid_spec,
        out_shape=jax.ShapeDtypeStruct((n_seq, qrows, SB_DH), BF16),
        compiler_params=pltpu.CompilerParams(dimension_semantics=("arbitrary",),
                                             vmem_limit_bytes=VMEM_LIMIT),
        name="sb_sample",
    )(page_table.reshape(-1), bias_rows, q_s, k_new, v_new, *([pool_kt] * (n_par * n_pages)),
      *([pool_vt] * (n_par * n_pages)), norm, up)


def kernel(x_prompt, x_sample, cache_sb_k, cache_sb_v, state_gla, page_table, p_prompt, p_sample, f1_pre, f1_in, f1_out, f1_post, mix_pre, w_in, gate_w, gate_b, gla_norm, sb_norm, sb_bias, w_out, mix_post, f2_pre, f2_in, f2_out, f2_post, ple_pre, ple_w_gate, ple_w_up, ple_post):
    batch, seq, _ = x_prompt.shape
    n_seq, n_new, _ = x_sample.shape
    depth = f1_in.shape[0]
    n_p, n_s = batch * seq, n_seq * n_new
    assert n_p % TOK_TILE == 0 and n_s % TOK_TILE == 0 and seq % SB_TILE == 0 and n_p % GLA_S_ROWS == 0

    assert seq % TOK_TILE == 0
    x_p, x_s = x_prompt.reshape(n_p, D_MODEL), x_sample.reshape(n_s, D_MODEL)
    vec = lambda a: a.reshape(1, -1)
    lr0 = ZG_W
    sb0 = ZG_W + GLA_RANK
    pk, pv, ps, sk_l, sv_l, ss = [], [], [], [], [], []
    for l in range(depth):
        wcat = jnp.concatenate(
            [w_in[l][:, :lr0], w_in[l][:, sb0:], w_in[l][:, lr0:sb0],
             jnp.zeros((D_MODEL, LANES - GLA_RANK), F32)], axis=1).astype(BF16)
        gw_pad = jnp.concatenate([gate_w[l], jnp.zeros((LANES - GLA_RANK, GLA_QK_W), F32)], axis=0)
        x1, zg, la, sq_b, sk_b, sv_b, sk_t, sv_t, sk_s, sv_s = _ffn_in(
            x_p, x_s, batch, vec(f1_pre[l]), f1_in[l].astype(BF16), f1_out[l].astype(BF16), vec(f1_post[l]),
            vec(mix_pre[l]), wcat, gw_pad, vec(gate_b[l]))

        a_p, s_gla_p = _gla_prompt(zg, la, vec(gla_norm[l]), batch, seq)
        a_p = a_p.reshape(n_p, GLA_V_W)
        bias2 = sb_bias[l] * LOG2E
        o_sb_p = _sb_prompt(bias2, sq_b, sk_b, sv_b, vec(jnp.tile(sb_norm[l], SB_HPS)), batch, seq)

        a_s, s_gla_s = _gla_sample(zg, la, state_gla[l:l + 1], vec(gla_norm[l]), n_p, n_seq, n_new)
        rows8 = lambda a: jnp.pad(a.astype(F32).reshape(n_seq, n_new, SB_W),
                                  ((0, 0), (0, SUBLANES - n_new), (0, 0)))
        bias_rows = jnp.broadcast_to(jnp.tile(bias2, n_new)[:, None], (n_new * SB_HEADS, LANES))
        to_hdk = lambda c: jnp.transpose(c, (0, 1, 3, 4, 2))
        o_sb_s = _sb_sample(page_table, bias_rows, rows8(sq_b[n_p:]), rows8(sk_s), rows8(sv_s),
                            to_hdk(cache_sb_k), to_hdk(cache_sb_v), vec(sb_norm[l]), l, n_new)

        x_p, x_s = _merge_out(
            x1, a_p, a_s, o_sb_p, o_sb_s.reshape(n_s, SB_W), p_prompt[l].reshape(n_p, PLE_DIM),
            p_sample[l].reshape(n_s, PLE_DIM), w_out[l].astype(BF16), vec(mix_post[l]), vec(f2_pre[l]),
            f2_in[l].astype(BF16), f2_out[l].astype(BF16), vec(f2_post[l]), vec(ple_pre[l]),
            ple_w_gate[l].astype(BF16), ple_w_up[l].astype(BF16), vec(ple_post[l]))

        pk.append(jnp.transpose(sk_t, (0, 3, 1, 2)))
        pv.append(jnp.transpose(sv_t, (0, 3, 1, 2)))
        ps.append(s_gla_p)
        sk_l.append(sk_s.reshape(n_seq, n_new, SB_HEADS, SB_DH))
        sv_l.append(sv_s.reshape(n_seq, n_new, SB_HEADS, SB_DH))
        ss.append(s_gla_s[0])
    return (x_p.reshape(batch, seq, D_MODEL), x_s.reshape(n_seq, n_new, D_MODEL),
            jnp.stack(pk), jnp.stack(pv), jnp.stack(ps), jnp.stack(sk_l), jnp.stack(sv_l), jnp.stack(ss))
```
